```python
import math
import jax
import jax.numpy as jnp
from jax import lax
import numpy as np

D_MODEL = 1024
BATCH = 8
SEQ = 4096
DEPTH = 2

GRID_W = 64
CTX_LEN = 256
HEAD_DIM = 64
ROPE_THETA = 10000.0
ROPE_PAIRS_PER_AXIS = HEAD_DIM // 4
QBLOCK = 128
WINDOW = 128
A_HEADS = 8
A_KV_HEADS = 2
B_HEADS = 4
B_V_DIM = 2 * HEAD_DIM
C_HEADS = 8
C_KV_HEADS = 2
N_BRANCHES = 3
A_WIDTH = A_HEADS * HEAD_DIM
B_WIDTH = B_HEADS * B_V_DIM
C_WIDTH = C_HEADS * HEAD_DIM
Q_SIZES = (A_HEADS * HEAD_DIM, B_HEADS * 2 * HEAD_DIM, C_HEADS * HEAD_DIM, N_BRANCHES * D_MODEL)
KV_SIZES = (A_KV_HEADS * HEAD_DIM, A_KV_HEADS * HEAD_DIM, B_HEADS * 2 * HEAD_DIM, B_HEADS * B_V_DIM,
            C_KV_HEADS * HEAD_DIM, C_KV_HEADS * HEAD_DIM)
Q_COLS = sum(Q_SIZES)
IN_COLS = Q_COLS + sum(KV_SIZES)
N_GROUPS = 4
EXPERTS_PER_GROUP = 8
N_EXPERTS = N_GROUPS * EXPERTS_PER_GROUP
TOP_K = 2
EXPERT_HIDDEN = D_MODEL // 2
N_MOD = 6
EPS = 1e-6
NEG_INF = -1e30

kernel_name = "hybrid_diffusion_backbone_ctx_prefix"


def rms_norm(x, gain=None):
    xf = x.astype(jnp.float32)
    y = xf * lax.rsqrt(jnp.mean(xf * xf, axis=-1, keepdims=True) + EPS)
    if gain is not None:
        y = y * gain.astype(jnp.float32)
    return y.astype(x.dtype)


def modulate(x, shift, scale):
    return x * (1 + scale) + shift


def split_cols(t, sizes):
    return jnp.split(t, np.cumsum(sizes)[:-1].tolist(), axis=-1)


def axial_rope_tables(n_tokens):
    rows = n_tokens // GRID_W
    row = jnp.repeat(jnp.arange(rows, dtype=jnp.float32), GRID_W)
    col = jnp.tile(jnp.arange(GRID_W, dtype=jnp.float32), rows)
    freqs = ROPE_THETA ** (-jnp.arange(ROPE_PAIRS_PER_AXIS, dtype=jnp.float32) / ROPE_PAIRS_PER_AXIS)
    ang = jnp.concatenate([row[:, None] * freqs, col[:, None] * freqs], axis=-1)
    return jnp.cos(ang), jnp.sin(ang)


def apply_rope(x, rope):
    cos, sin = rope
    shape = (cos.shape[0],) + (1,) * (x.ndim - 3) + (cos.shape[1],)
    cos, sin = cos.reshape(shape), sin.reshape(shape)
    half = HEAD_DIM // 2
    xf = x.astype(jnp.float32)
    x1, x2 = xf[..., :half], xf[..., half:]
    return jnp.concatenate([x1 * cos - x2 * sin, x1 * sin + x2 * cos], axis=-1).astype(x.dtype)


def split_queries(qg, P, rope):
    aq, bq, cq, gates = split_cols(qg, Q_SIZES)
    n, t = qg.shape[:2]
    aq = rms_norm(aq.reshape(n, t, A_HEADS, HEAD_DIM), P["a_qnorm"])
    bq = rms_norm(bq.reshape(n, t, B_HEADS, 2, HEAD_DIM), P["b_qnorm"])
    cq = rms_norm(cq.reshape(n, t, C_HEADS, HEAD_DIM), P["c_qnorm"])
    if rope is not None:
        aq, bq, cq = apply_rope(aq, rope), apply_rope(bq, rope), apply_rope(cq, rope)
    aq = aq.reshape(n, t, A_KV_HEADS, A_HEADS // A_KV_HEADS, HEAD_DIM)
    cq = cq.reshape(n, t, C_KV_HEADS, C_HEADS // C_KV_HEADS, HEAD_DIM)
    return aq, bq, cq, gates


def split_keys_values(kv, P, rope):
    ak, av, bk, bv, ck, cv = split_cols(kv, KV_SIZES)
    n, s = kv.shape[:2]
    ak = rms_norm(ak.reshape(n, s, A_KV_HEADS, HEAD_DIM), P["a_knorm"])
    bk = rms_norm(bk.reshape(n, s, B_HEADS, 2, HEAD_DIM), P["b_knorm"])
    ck = rms_norm(ck.reshape(n, s, C_KV_HEADS, HEAD_DIM), P["c_knorm"])
    if rope is not None:
        ak, bk, ck = apply_rope(ak, rope), apply_rope(bk, rope), apply_rope(ck, rope)
    av = av.reshape(n, s, A_KV_HEADS, HEAD_DIM)
    bv = bv.reshape(n, s, B_HEADS, B_V_DIM)
    cv = cv.reshape(n, s, C_KV_HEADS, HEAD_DIM)
    return ak, av, bk, bv, ck, cv


def gqa_attend(q, k, v, mask=None, sink=None):
    s = jnp.einsum("btkgd,bskd->bkgts", q, k).astype(jnp.float32) * (q.shape[-1] ** -0.5)
    if mask is not None:
        s = jnp.where(mask, s, NEG_INF)
    if sink is not None:
        snk = jnp.broadcast_to(sink.astype(jnp.float32).reshape(1, k.shape[2], -1, 1, 1), s.shape[:-1] + (1,))
        p = jax.nn.softmax(jnp.concatenate([s, snk], axis=-1), axis=-1)[..., :-1]
    else:
        p = jax.nn.softmax(s, axis=-1)
    o = jnp.einsum("bkgts,bskd->btkgd", p.astype(v.dtype), v)
    return o.reshape(o.shape[:2] + (-1,))


def diff_attend(q, k, v, lam, subln_gain, lam_init):
    s = jnp.einsum("btncd,bsncd->bncts", q, k).astype(jnp.float32) * (q.shape[-1] ** -0.5)
    p = jax.nn.softmax(s, axis=-1)
    a = p[:, :, 0] - lam * p[:, :, 1]
    o = jnp.einsum("bnts,bsne->btne", a.astype(v.dtype), v)
    o = rms_norm(o, subln_gain) * (1.0 - lam_init)
    return o.reshape(o.shape[:2] + (-1,))


def sweep_query_blocks(attend_block, q):
    n, length = q.shape[:2]
    nb = length // QBLOCK
    qb = jnp.moveaxis(q.reshape((n, nb, QBLOCK) + q.shape[2:]), 1, 0)
    o = lax.map(lambda a: attend_block(a[0], a[1]), (jnp.arange(nb), qb))
    return jnp.moveaxis(o, 0, 1).reshape(n, length, -1)


def windowed_latent(q, k_lat, v_lat, k_ctx, v_ctx, sink):
    length = k_lat.shape[1]
    span = QBLOCK + 2 * WINDOW
    pad = ((0, 0), (WINDOW, WINDOW), (0, 0), (0, 0))
    kp, vp = jnp.pad(k_lat, pad), jnp.pad(v_lat, pad)
    rel = jnp.arange(span)[None, :] - WINDOW - jnp.arange(QBLOCK)[:, None]
    band = jnp.abs(rel) <= WINDOW
    ctx_cols = jnp.ones((QBLOCK, k_ctx.shape[1]), dtype=bool)

    def block(b, qb):
        start = b * QBLOCK
        kb = lax.dynamic_slice_in_dim(kp, start, span, axis=1)
        vb = lax.dynamic_slice_in_dim(vp, start, span, axis=1)
        kpos = start - WINDOW + jnp.arange(span)
        valid = band & ((kpos >= 0) & (kpos < length))[None, :]
        mask = jnp.concatenate([valid, ctx_cols], axis=-1)
        return gqa_attend(qb, jnp.concatenate([kb, k_ctx], axis=1), jnp.concatenate([vb, v_ctx], axis=1), mask, sink)

    return sweep_query_blocks(block, q)


def merge_branches(ya, yb, yc, gates, P):
    g = jax.nn.sigmoid(gates.astype(jnp.float32)).astype(ya.dtype)
    ga, gb, gc = jnp.split(g, N_BRANCHES, axis=-1)
    m = ga * (ya @ P["w_branch_a"]) + gb * (yb @ P["w_branch_b"]) + gc * (yc @ P["w_branch_c"])
    return m @ P["w_out"]


def hier_moe(h, w_rg, w_re, w_gate, w_up, w_down):
    n = h.shape[0]
    g_logits = (h @ w_rg).astype(jnp.float32)
    g_prob = jax.nn.softmax(g_logits, axis=-1)
    g_idx = jnp.argmax(g_logits, axis=-1)
    g_w = jnp.take_along_axis(g_prob, g_idx[:, None], axis=-1)
    e_logits = (h @ w_re).astype(jnp.float32).reshape(n, N_GROUPS, EXPERTS_PER_GROUP)
    e_in_group = jnp.take_along_axis(e_logits, g_idx[:, None, None], axis=1)[:, 0]
    top_v, top_i = lax.top_k(e_in_group, TOP_K)
    w = jax.nn.softmax(top_v, axis=-1) * g_w
    flat_e = (g_idx[:, None] * EXPERTS_PER_GROUP + top_i).reshape(-1)
    order = jnp.argsort(flat_e)
    tok = order // TOP_K
    sizes = jnp.bincount(flat_e, length=N_EXPERTS).astype(jnp.int32)
    xs = h[tok]
    a = lax.ragged_dot(xs, w_gate, sizes)
    u = lax.ragged_dot(xs, w_up, sizes)
    y = lax.ragged_dot(jax.nn.silu(a) * u, w_down, sizes)
    y = y * w.reshape(-1)[order][:, None].astype(y.dtype)
    return jax.ops.segment_sum(y, tok, num_segments=n)


def hybrid_layer(l, x, ctx, silu_c, silu_c_ctx, P, rope, need_ctx):
    n, length, d = x.shape
    c_len = ctx.shape[1]
    mod = (silu_c @ P["w_ada"] + P["b_ada"]).reshape(n, N_MOD, 1, d)
    cmod = (silu_c_ctx @ P["w_ada"] + P["b_ada"]).reshape(N_MOD, 1, 1, d)
    lam_init = 0.8 - 0.6 * math.exp(-0.3 * l)
    lam = (jnp.exp(jnp.sum(P["lambda_q1"].astype(jnp.float32) * P["lambda_k1"].astype(jnp.float32)))
           - jnp.exp(jnp.sum(P["lambda_q2"].astype(jnp.float32) * P["lambda_k2"].astype(jnp.float32)))
           + lam_init)

    h = modulate(rms_norm(x), mod[:, 0], mod[:, 1])
    hc = modulate(rms_norm(ctx), cmod[0], cmod[1])
    proj = h @ P["w_in"]
    aq, bq, cq, gates = split_queries(proj[..., :Q_COLS], P, rope)
    ak, av, bk, bv, ck, cv = split_keys_values(proj[..., Q_COLS:], P, rope)
    if need_ctx:
        cproj = hc @ P["w_in"]
        ckv_in = cproj[..., Q_COLS:]
    else:
        ckv_in = hc @ P["w_in"][:, Q_COLS:]
    xak, xav, xbk, xbv, xck, xcv = split_keys_values(ckv_in, P, None)

    k_a, v_a = jnp.concatenate([ak, xak], axis=1), jnp.concatenate([av, xav], axis=1)
    k_b, v_b = jnp.concatenate([bk, xbk], axis=1), jnp.concatenate([bv, xbv], axis=1)
    ya = sweep_query_blocks(lambda b, qb: gqa_attend(qb, k_a, v_a), aq)
    yb = sweep_query_blocks(lambda b, qb: diff_attend(qb, k_b, v_b, lam, P["b_subln"], lam_init), bq)
    yc = windowed_latent(cq, ck, cv, xck, xcv, P["c_sink"])
    x = x + mod[:, 2] * merge_branches(ya, yb, yc, gates, P)

    if need_ctx:
        caq, cbq, ccq, cgates = split_queries(cproj[..., :Q_COLS], P, None)
        cya = gqa_attend(caq, xak, xav)
        cyb = diff_attend(cbq, xbk, xbv, lam, P["b_subln"], lam_init)
        cyc = gqa_attend(ccq, xck, xcv, sink=P["c_sink"])
        ctx = ctx + cmod[2] * merge_branches(cya, cyb, cyc, cgates, P)

    h2 = modulate(rms_norm(x), mod[:, 3], mod[:, 4])
    tokens = h2.reshape(n * length, d)
    if need_ctx:
        hc2 = modulate(rms_norm(ctx), cmod[3], cmod[4])
        tokens = jnp.concatenate([tokens, hc2.reshape(n * c_len, d)], axis=0)
    f = hier_moe(tokens, P["w_router_group"], P["w_router_expert"], P["w_exp_gate"], P["w_exp_up"], P["w_exp_down"])
    f = f.astype(x.dtype)
    x = x + mod[:, 5] * f[: n * length].reshape(n, length, d)
    if need_ctx:
        ctx = ctx + cmod[5] * f[n * length:].reshape(n, c_len, d)
    return x, ctx


def setup_inputs(seed: int = 0) -> dict:
    key = jax.random.key(seed)
    ks = jax.random.split(key, 28)
    f32 = jnp.float32
    nrm = lambda k, shape, s: jax.random.normal(k, shape, f32) * s
    gain = lambda k, shape: 1.0 + 0.05 * jax.random.normal(k, shape, f32)
    D = D_MODEL
    return {
        "x": nrm(ks[0], (BATCH, SEQ, D), 1.0),
        "c": nrm(ks[1], (BATCH, D), 1.0),
        "ctx": nrm(ks[2], (BATCH, CTX_LEN, D), 1.0),
        "c_ctx": nrm(ks[3], (D,), 1.0),
        "w_ada": nrm(ks[4], (DEPTH, D, N_MOD * D), 0.5 * D ** -0.5),
        "b_ada": nrm(ks[5], (DEPTH, N_MOD * D), 0.02),
        "w_in": nrm(ks[6], (DEPTH, D, IN_COLS), D ** -0.5),
        "a_qnorm": gain(ks[7], (DEPTH, HEAD_DIM)),
        "a_knorm": gain(ks[8], (DEPTH, HEAD_DIM)),
        "b_qnorm": gain(ks[9], (DEPTH, HEAD_DIM)),
        "b_knorm": gain(ks[10], (DEPTH, HEAD_DIM)),
        "c_qnorm": gain(ks[11], (DEPTH, HEAD_DIM)),
        "c_knorm": gain(ks[12], (DEPTH, HEAD_DIM)),
        "lambda_q1": nrm(ks[13], (DEPTH, HEAD_DIM), 0.1),
        "lambda_k1": nrm(ks[14], (DEPTH, HEAD_DIM), 0.1),
        "lambda_q2": nrm(ks[15], (DEPTH, HEAD_DIM), 0.1),
        "lambda_k2": nrm(ks[16], (DEPTH, HEAD_DIM), 0.1),
        "b_subln": gain(ks[17], (DEPTH, B_V_DIM)),
        "c_sink": nrm(ks[18], (DEPTH, C_HEADS), 0.5),
        "w_branch_a": nrm(ks[19], (DEPTH, A_WIDTH, D), A_WIDTH ** -0.5),
        "w_branch_b": nrm(ks[20], (DEPTH, B_WIDTH, D), B_WIDTH ** -0.5),
        "w_branch_c": nrm(ks[21], (DEPTH, C_WIDTH, D), C_WIDTH ** -0.5),
        "w_out": nrm(ks[22], (DEPTH, D, D), D ** -0.5),
        "w_router_group": nrm(ks[23], (DEPTH, D, N_GROUPS), D ** -0.5),
        "w_router_expert": nrm(ks[24], (DEPTH, D, N_EXPERTS), D ** -0.5),
        "w_exp_gate": nrm(ks[25], (DEPTH, N_EXPERTS, D, EXPERT_HIDDEN), D ** -0.5),
        "w_exp_up": nrm(ks[26], (DEPTH, N_EXPERTS, D, EXPERT_HIDDEN), D ** -0.5),
        "w_exp_down": nrm(ks[27], (DEPTH, N_EXPERTS, EXPERT_HIDDEN, D), EXPERT_HIDDEN ** -0.5),
    }


def reference(x, c, ctx, c_ctx, w_ada, b_ada, w_in, a_qnorm, a_knorm, b_qnorm, b_knorm, c_qnorm, c_knorm,
              lambda_q1, lambda_k1, lambda_q2, lambda_k2, b_subln, c_sink, w_branch_a, w_branch_b, w_branch_c,
              w_out, w_router_group, w_router_expert, w_exp_gate, w_exp_up, w_exp_down):
    rope = axial_rope_tables(x.shape[1])
    silu_c = jax.nn.silu(c)
    silu_c_ctx = jax.nn.silu(c_ctx)
    for l in range(DEPTH):
        P = {
            "w_ada": w_ada[l], "b_ada": b_ada[l], "w_in": w_in[l],
            "a_qnorm": a_qnorm[l], "a_knorm": a_knorm[l], "b_qnorm": b_qnorm[l], "b_knorm": b_knorm[l],
            "c_qnorm": c_qnorm[l], "c_knorm": c_knorm[l],
            "lambda_q1": lambda_q1[l], "lambda_k1": lambda_k1[l], "lambda_q2": lambda_q2[l], "lambda_k2": lambda_k2[l],
            "b_subln": b_subln[l], "c_sink": c_sink[l],
            "w_branch_a": w_branch_a[l], "w_branch_b": w_branch_b[l], "w_branch_c": w_branch_c[l], "w_out": w_out[l],
            "w_router_group": w_router_group[l], "w_router_expert": w_router_expert[l],
            "w_exp_gate": w_exp_gate[l], "w_exp_up": w_exp_up[l], "w_exp_down": w_exp_down[l],
        }
        x, ctx = hybrid_layer(l, x, ctx, silu_c, silu_c_ctx, P, rope, l < DEPTH - 1)
    return x
```

```python
import functools
import math

import jax
import jax.numpy as jnp
from jax import lax
from jax.experimental import pallas as pl
from jax.experimental.pallas import tpu as pltpu

F32 = jnp.float32
BF16 = jnp.bfloat16

D_MODEL = 1024
HEAD_DIM = 64
GRID_W = 64
ROPE_THETA = 10000.0
WINDOW = 128
N_GROUPS = 4
EXPERTS_PER_GROUP = 8
N_EXPERTS = N_GROUPS * EXPERTS_PER_GROUP
EXPERT_HIDDEN = D_MODEL // 2
N_MOD = 6
EPS = 1e-6
NEG = -1e30
LANES = 128
MOD_ROWS = 16

QK_COLS = 2304
G_COLS = 3 * D_MODEL
V_COLS = 768
NORM_CHUNK = 256

VMEM_LIMIT = 48 * 1024 * 1024


def _params(n_axes, vmem=VMEM_LIMIT):
    return pltpu.CompilerParams(dimension_semantics=("arbitrary",) * n_axes, vmem_limit_bytes=vmem)


def _resident(shape):
    return pl.BlockSpec(shape, lambda *_: (0,) * len(shape), pipeline_mode=pl.Buffered(1))


def _sigmoid(v):
    return 1.0 / (1.0 + jnp.exp(-v))


def _rms_modulate(v, shift, scale):
    v = v * lax.rsqrt(jnp.mean(v * v, axis=-1, keepdims=True) + EPS)
    return v * (1.0 + scale) + shift


def _ada_kernel(c_ref, w_ref, b_ref, o_ref):
    c = c_ref[...]
    o_ref[0] = jnp.dot(c * _sigmoid(c), w_ref[0], preferred_element_type=F32,
                       precision=lax.Precision.HIGHEST) + b_ref[0]


def _ada(c_rows, w_ada, b_ada):
    depth, d, n = w_ada.shape
    tn = 1536
    return pl.pallas_call(
        _ada_kernel,
        grid=(depth, n // tn),
        in_specs=[pl.BlockSpec((MOD_ROWS, d), lambda l, j: (0, 0)),
                  pl.BlockSpec((1, d, tn), lambda l, j: (l, 0, j)),
                  pl.BlockSpec((1, 1, tn), lambda l, j: (l, 0, j))],
        out_specs=pl.BlockSpec((1, MOD_ROWS, tn), lambda l, j: (l, 0, j)),
        out_shape=jax.ShapeDtypeStruct((depth, MOD_ROWS, n), F32),
        compiler_params=_params(2),
        name="ada",
    )(c_rows, w_ada, b_ada)


def _rope_partner(z):
    lane = lax.broadcasted_iota(jnp.int32, z.shape, 1)
    return jnp.where((lane & 32) == 0, pltpu.roll(z, 96, 1), pltpu.roll(z, 32, 1))


def _inproj_kernel(x_ref, shift_ref, scale_ref, wqk_ref, wg_ref, wv_ref, gain_ref, ones_ref, cos_ref, sin_ref,
                   oqk_ref, og_ref, ov_ref, *, use_rope):
    hb = _rms_modulate(x_ref[0], shift_ref[0], scale_ref[0]).astype(BF16)
    for c in range(G_COLS // 512):
        cols = slice(c * 512, (c + 1) * 512)
        og_ref[0, :, cols] = _sigmoid(jnp.dot(hb, wg_ref[:, cols], preferred_element_type=F32)).astype(BF16)
    ov_ref[0] = jnp.dot(hb, wv_ref[...], preferred_element_type=F32).astype(BF16)
    for c in range(QK_COLS // NORM_CHUNK):
        cols = slice(c * NORM_CHUNK, (c + 1) * NORM_CHUNK)
        y = jnp.dot(hb, wqk_ref[:, cols], preferred_element_type=F32)
        ss = jnp.dot((y * y).astype(BF16), ones_ref[...], preferred_element_type=F32)
        yn = y * lax.rsqrt(ss * (1.0 / HEAD_DIM) + EPS) * gain_ref[:, cols]
        if use_rope:
            halves = []
            for k in range(NORM_CHUNK // LANES):
                z = yn[:, k * LANES:(k + 1) * LANES]
                halves.append(z * cos_ref[...] + _rope_partner(z) * sin_ref[...])
            yn = jnp.concatenate(halves, axis=1)
        oqk_ref[0, :, cols] = yn.astype(BF16)


def _inproj(x, shift, scale, wqk, wg, wv, gain, ones, cos, sin, *, use_rope, tm):
    bx, t, d = x.shape
    row = lambda b, i: (b, 0, 0)
    tile = lambda b, i: (b, i, 0)
    return pl.pallas_call(
        functools.partial(_inproj_kernel, use_rope=use_rope),
        grid=(bx, t // tm),
        in_specs=[pl.BlockSpec((1, tm, d), tile),
                  pl.BlockSpec((1, 1, d), row), pl.BlockSpec((1, 1, d), row),
                  _resident((d, QK_COLS)), _resident((d, G_COLS)), _resident((d, V_COLS)),
                  _resident((1, QK_COLS)), _resident((NORM_CHUNK, NORM_CHUNK)),
                  pl.BlockSpec((tm, LANES), lambda b, i: (i, 0)), pl.BlockSpec((tm, LANES), lambda b, i: (i, 0))],
        out_specs=[pl.BlockSpec((1, tm, QK_COLS), tile), pl.BlockSpec((1, tm, G_COLS), tile),
                   pl.BlockSpec((1, tm, V_COLS), tile)],
        out_shape=[jax.ShapeDtypeStruct((bx, t, QK_COLS), BF16), jax.ShapeDtypeStruct((bx, t, G_COLS), BF16),
                   jax.ShapeDtypeStruct((bx, t, V_COLS), BF16)],
        compiler_params=_params(2),
        name="inproj_rope" if use_rope else "inproj_ctx",
    )(x, shift, scale, wqk, wg, wv, gain, ones, cos, sin)


def _qk(qs, k):
    return lax.dot_general(qs, k, (((1,), (1,)), ((), ())), preferred_element_type=F32)


def _online_softmax(qs, kl_ref, vl_ref, kc_ref, vc_ref, lanes, n_lat, tk, m0, l0):
    def step(k, v, carry):
        m, l, acc = carry
        s = _qk(qs, k)
        m_new = jnp.maximum(m, jnp.max(s, axis=-1, keepdims=True))
        alpha = jnp.exp(m - m_new)
        p = jnp.exp(s - m_new)
        l = alpha * l + jnp.sum(p, axis=-1, keepdims=True)
        acc = alpha * acc + jnp.dot(p.astype(BF16), v, preferred_element_type=F32)
        return m_new, l, acc

    carry = (m0, l0, jnp.zeros((qs.shape[0], LANES), F32))
    if n_lat:
        def body(j, carry):
            rows = pl.ds(pl.multiple_of(j * tk, tk), tk)
            return step(kl_ref[0, rows, lanes], vl_ref[0, rows, lanes], carry)
        carry = lax.fori_loop(0, n_lat, body, carry)
    _, l, acc = step(kc_ref[0, :, lanes], vc_ref[0, :, lanes], carry)
    return acc / l


def _stack_group_heads(q_ref, g, tq):
    lane = lax.broadcasted_iota(jnp.int32, (tq, LANES), 1)
    keep = (lane >= HEAD_DIM) if g else (lane < HEAD_DIM)
    parts = []
    for h in range(4 * g, 4 * g + 4):
        z = q_ref[0, :, (h // 2) * LANES:(h // 2 + 1) * LANES].astype(F32)
        if h % 2 != g:
            z = pltpu.roll(z, HEAD_DIM, 1)
        parts.append(jnp.where(keep, z, 0.0).astype(BF16))
    return jnp.concatenate(parts, axis=0)


def _unstack_group_heads(o, g, tq):
    lane = lax.broadcasted_iota(jnp.int32, (tq, LANES), 1)
    outs = []
    for c in range(2):
        a = o[(2 * c) * tq:(2 * c + 1) * tq]
        b = o[(2 * c + 1) * tq:(2 * c + 2) * tq]
        if g == 0:
            b = pltpu.roll(b, HEAD_DIM, 1)
        else:
            a = pltpu.roll(a, HEAD_DIM, 1)
        outs.append(jnp.where(lane < HEAD_DIM, a, b))
    return jnp.concatenate(outs, axis=1)


def _sink_rows(sink_ref, g, tq):
    return jnp.concatenate([jnp.full((tq, 1), sink_ref[4 * g + i], F32) for i in range(4)], axis=0)


def _gqa_kernel(sink_ref, q_ref, kl_ref, vl_ref, kc_ref, vc_ref, o_ref, *, tq, tk, n_lat, use_sink):
    for g in range(2):
        qs = _stack_group_heads(q_ref, g, tq)
        if use_sink:
            m0, l0 = _sink_rows(sink_ref, g, tq), jnp.ones((4 * tq, 1), F32)
        else:
            m0, l0 = jnp.full((4 * tq, 1), NEG, F32), jnp.zeros((4 * tq, 1), F32)
        o = _online_softmax(qs, kl_ref, vl_ref, kc_ref, vc_ref, slice(0, LANES), n_lat, tk, m0, l0)
        o_ref[0, :, g * 256:(g + 1) * 256] = _unstack_group_heads(o, g, tq).astype(BF16)


def _diff_kernel(lam_ref, q_ref, kl_ref, vl_ref, kc_ref, vc_ref, gain_ref, o_ref, *, tq, tk, n_lat, out_scale):
    lane = lax.broadcasted_iota(jnp.int32, (tq, LANES), 1)
    for n in range(4):
        lanes = slice(n * LANES, (n + 1) * LANES)
        z = q_ref[0, :, lanes]
        zero = jnp.zeros_like(z)
        qs = jnp.concatenate([jnp.where(lane < HEAD_DIM, z, zero), jnp.where(lane >= HEAD_DIM, z, zero)], axis=0)
        m0, l0 = jnp.full((2 * tq, 1), NEG, F32), jnp.zeros((2 * tq, 1), F32)
        o = _online_softmax(qs, kl_ref, vl_ref, kc_ref, vc_ref, lanes, n_lat, tk, m0, l0)
        d = o[:tq] - lam_ref[0] * o[tq:]
        d = d * lax.rsqrt(jnp.mean(d * d, axis=-1, keepdims=True) + EPS) * gain_ref[...] * out_scale
        o_ref[0, :, lanes] = d.astype(BF16)


def _window_kernel(sink_ref, q_ref, kl_ref, vl_ref, kc_ref, vc_ref, o_ref, *, tq, length):
    i = pl.program_id(1)
    span = tq + 2 * WINDOW
    start = pl.multiple_of(jnp.clip(i * tq - WINDOW, 0, length - span), LANES)
    kw = kl_ref[0, pl.ds(start, span), :]
    vw = vl_ref[0, pl.ds(start, span), :]
    kc = kc_ref[0]
    vc = vc_ref[0]
    rows = 4 * tq
    qpos = i * tq + (lax.broadcasted_iota(jnp.int32, (rows, span), 0) & (tq - 1))
    kpos = start + lax.broadcasted_iota(jnp.int32, (rows, span), 1)
    valid = jnp.abs(qpos - kpos) <= WINDOW
    for g in range(2):
        qs = _stack_group_heads(q_ref, g, tq)
        sw = jnp.where(valid, _qk(qs, kw), NEG)
        sc = _qk(qs, kc)
        snk = _sink_rows(sink_ref, g, tq)
        m = jnp.maximum(jnp.maximum(jnp.max(sw, axis=-1, keepdims=True), jnp.max(sc, axis=-1, keepdims=True)), snk)
        pw = jnp.exp(sw - m)
        pc = jnp.exp(sc - m)
        l = jnp.sum(pw, axis=-1, keepdims=True) + jnp.sum(pc, axis=-1, keepdims=True) + jnp.exp(snk - m)
        o = (jnp.dot(pw.astype(BF16), vw, preferred_element_type=F32)
             + jnp.dot(pc.astype(BF16), vc, preferred_element_type=F32)) / l
        o_ref[0, :, g * 256:(g + 1) * 256] = _unstack_group_heads(o, g, tq).astype(BF16)


_SMEM = pl.BlockSpec(memory_space=pltpu.SMEM)


def _kv_specs(klat, kctx, k_blk, v_blk, width):
    s_lat, s_ctx = klat.shape[1], kctx.shape[1]
    return [pl.BlockSpec((1, s_lat, width), lambda b, i: (b, 0, k_blk)),
            pl.BlockSpec((1, s_lat, width), lambda b, i: (b, 0, v_blk)),
            pl.BlockSpec((1, s_ctx, width), lambda b, i: (b, 0, k_blk)),
            pl.BlockSpec((1, s_ctx, width), lambda b, i: (b, 0, v_blk))]


def _gqa(sink, q, klat, vlat, kctx, vctx, *, q_blk, k_blk, v_blk, tq, tk, n_lat, use_sink, name):
    bx, t, _ = q.shape
    return pl.pallas_call(
        functools.partial(_gqa_kernel, tq=tq, tk=tk, n_lat=n_lat, use_sink=use_sink),
        grid=(bx, t // tq),
        in_specs=[_SMEM, pl.BlockSpec((1, tq, 512), lambda b, i: (b, i, q_blk))]
        + _kv_specs(klat, kctx, k_blk, v_blk, LANES),
        out_specs=pl.BlockSpec((1, tq, 512), lambda b, i: (b, i, 0)),
        out_shape=jax.ShapeDtypeStruct((bx, t, 512), BF16),
        compiler_params=_params(2),
        name=name,
    )(sink, q, klat, vlat, kctx, vctx)


def _diff(lam, q, klat, vlat, kctx, vctx, gain, *, tq, tk, n_lat, out_scale, name):
    bx, t, _ = q.shape
    return pl.pallas_call(
        functools.partial(_diff_kernel, tq=tq, tk=tk, n_lat=n_lat, out_scale=out_scale),
        grid=(bx, t // tq),
        in_specs=[_SMEM, pl.BlockSpec((1, tq, 512), lambda b, i: (b, i, 1))]
        + _kv_specs(klat, kctx, 3, 0, 512) + [pl.BlockSpec((1, LANES), lambda b, i: (0, 0))],
        out_specs=pl.BlockSpec((1, tq, 512), lambda b, i: (b, i, 0)),
        out_shape=jax.ShapeDtypeStruct((bx, t, 512), BF16),
        compiler_params=_params(2),
        name=name,
    )(lam, q, klat, vlat, kctx, vctx, gain)


def _window(sink, q, klat, vlat, kctx, vctx, *, tq):
    bx, t, _ = q.shape
    return pl.pallas_call(
        functools.partial(_window_kernel, tq=tq, length=t),
        grid=(bx, t // tq),
        in_specs=[_SMEM, pl.BlockSpec((1, tq, 512), lambda b, i: (b, i, 2))]
        + _kv_specs(klat, kctx, 17, 5, LANES),
        out_specs=pl.BlockSpec((1, tq, 512), lambda b, i: (b, i, 0)),
        out_shape=jax.ShapeDtypeStruct((bx, t, 512), BF16),
        compiler_params=_params(2),
        name="attn_window",
    )(sink, q, klat, vlat, kctx, vctx)


def _route(logits):
    lane = lax.broadcasted_iota(jnp.int32, logits.shape, 1).astype(F32)
    first = lambda hit: jnp.min(jnp.where(hit, lane, float(LANES)), axis=-1, keepdims=True)
    gl = jnp.where(lane < N_GROUPS, logits, NEG)
    gmax = jnp.max(gl, axis=-1, keepdims=True)
    gidx = first(gl == gmax)
    gw = 1.0 / jnp.sum(jnp.exp(gl - gmax), axis=-1, keepdims=True)
    lo = N_GROUPS + gidx * EXPERTS_PER_GROUP
    el = jnp.where((lane >= lo) & (lane < lo + EXPERTS_PER_GROUP), logits, NEG)
    v1 = jnp.max(el, axis=-1, keepdims=True)
    i1 = first(el == v1)
    el = jnp.where(lane == i1, NEG, el)
    v2 = jnp.max(el, axis=-1, keepdims=True)
    i2 = first(el == v2)
    e = jnp.exp(v2 - v1)
    w1 = gw / (1.0 + e)
    w2 = gw * e / (1.0 + e)
    return jnp.where(lane == 0, i1 - N_GROUPS,
                     jnp.where(lane == 1, i2 - N_GROUPS, jnp.where(lane == 2, w1, jnp.where(lane == 3, w2, 0.0))))


def _merge_kernel(ya_ref, yb_ref, yc_ref, g_ref, x_ref, gate_ref, shift_ref, scale_ref,
                  wa_ref, wb_ref, wc_ref, wo_ref, wr_ref, xo_ref, h_ref, r_ref):
    d = D_MODEL
    m = None
    for k, (y_ref, w_ref) in enumerate(((ya_ref, wa_ref), (yb_ref, wb_ref), (yc_ref, wc_ref))):
        t = g_ref[0, :, k * d:(k + 1) * d].astype(F32) * jnp.dot(y_ref[0], w_ref[...], preferred_element_type=F32)
        m = t if m is None else m + t
    xn = x_ref[0] + gate_ref[0] * jnp.dot(m.astype(BF16), wo_ref[...], preferred_element_type=F32)
    xo_ref[0] = xn
    h2 = _rms_modulate(xn, shift_ref[0], scale_ref[0])
    h_ref[0] = h2
    r_ref[0] = _route(jnp.dot(h2, wr_ref[...], preferred_element_type=F32, precision=lax.Precision.HIGHEST))


def _merge(ya, yb, yc, g, x, gate, shift, scale, wa, wb, wc, wo, wr, *, tm, name):
    bx, t, d = x.shape
    row = lambda b, i: (b, 0, 0)
    tile = lambda b, i: (b, i, 0)
    return pl.pallas_call(
        _merge_kernel,
        grid=(bx, t // tm),
        in_specs=[pl.BlockSpec((1, tm, 512), tile)] * 3
        + [pl.BlockSpec((1, tm, G_COLS), tile), pl.BlockSpec((1, tm, d), tile)]
        + [pl.BlockSpec((1, 1, d), row)] * 3
        + [_resident((512, d))] * 3 + [_resident((d, d)), _resident((d, LANES))],
        out_specs=[pl.BlockSpec((1, tm, d), tile), pl.BlockSpec((1, tm, d), tile), pl.BlockSpec((1, tm, LANES), tile)],
        out_shape=[jax.ShapeDtypeStruct((bx, t, d), F32), jax.ShapeDtypeStruct((bx, t, d), F32),
                   jax.ShapeDtypeStruct((bx, t, LANES), F32)],
        compiler_params=_params(2),
        name=name,
    )(ya, yb, yc, g, x, gate, shift, scale, wa, wb, wc, wo, wr)


def _dispatch_kernel(dest_ref, h_ref, xs_in_ref, xs_ref, sem, *, tokens):
    del xs_in_ref

    def row_copy(r, d):
        return pltpu.make_async_copy(h_ref.at[pl.ds(r, 1), :], xs_ref.at[pl.ds(d, 1), :], sem)

    def issue(r, carry):
        for k in range(2):
            row_copy(r, dest_ref[0, 0, 2 * r + k]).start()
        return carry

    def drain(r, carry):
        for k in range(2):
            row_copy(r, dest_ref[0, 0, 2 * r + k]).wait()
        return carry

    lax.fori_loop(0, tokens, issue, 0)
    lax.fori_loop(0, tokens, drain, 0)


def _dispatch(dest, h, xs, *, td):
    n, d = h.shape
    return pl.pallas_call(
        functools.partial(_dispatch_kernel, tokens=td),
        grid=(n // td,),
        in_specs=[pl.BlockSpec((1, 1, 2 * td), lambda i: (i, 0, 0), memory_space=pltpu.SMEM),
                  pl.BlockSpec((td, d), lambda i: (i, 0)),
                  pl.BlockSpec(memory_space=pl.ANY)],
        out_specs=pl.BlockSpec(memory_space=pl.ANY),
        out_shape=jax.ShapeDtypeStruct(xs.shape, xs.dtype),
        scratch_shapes=[pltpu.SemaphoreType.DMA(())],
        input_output_aliases={2: 0},
        compiler_params=_params(1),
        name="moe_dispatch",
    )(dest.reshape(n // td, 1, 2 * td), h, xs)


def _experts_kernel(te_ref, used_ref, xs_ref, wg_ref, wu_ref, wd_ref, y_ref):
    del te_ref
    t = pl.program_id(0)

    @pl.when(t < used_ref[0])
    def _():
        xb = xs_ref[...].astype(BF16)
        a = jnp.dot(xb, wg_ref[0], preferred_element_type=F32)
        u = jnp.dot(xb, wu_ref[0], preferred_element_type=F32)
        y_ref[...] = jnp.dot((a * _sigmoid(a) * u).astype(BF16), wd_ref[0], preferred_element_type=F32)

    @pl.when(t >= used_ref[0])
    def _():
        y_ref[...] = jnp.zeros_like(y_ref)


def _experts(tile_expert, n_used, xs, wg, wu, wd, *, tm):
    p, d = xs.shape
    hid = wg.shape[2]
    return pl.pallas_call(
        _experts_kernel,
        grid_spec=pltpu.PrefetchScalarGridSpec(
            num_scalar_prefetch=2,
            grid=(p // tm,),
            in_specs=[pl.BlockSpec((tm, d), lambda t, te, nu: (t, 0)),
                      pl.BlockSpec((1, d, hid), lambda t, te, nu: (te[t], 0, 0)),
                      pl.BlockSpec((1, d, hid), lambda t, te, nu: (te[t], 0, 0)),
                      pl.BlockSpec((1, hid, d), lambda t, te, nu: (te[t], 0, 0))],
            out_specs=pl.BlockSpec((tm, d), lambda t, te, nu: (t, 0))),
        out_shape=jax.ShapeDtypeStruct((p, d), F32),
        compiler_params=_params(1),
        name="moe_experts",
    )(tile_expert, n_used, xs, wg, wu, wd)


def _combine_kernel(dest_ref, y_ref, x_ref, gate_ref, r_ref, o_ref, buf, sem, *, tokens):
    def row_copy(r, k, d):
        return pltpu.make_async_copy(y_ref.at[pl.ds(d, 1), :], buf.at[k, pl.ds(r, 1), :], sem)

    def issue(r, carry):
        for k in range(2):
            row_copy(r, k, dest_ref[0, 0, 2 * r + k]).start()
        return carry

    def drain(r, carry):
        for k in range(2):
            row_copy(r, k, dest_ref[0, 0, 2 * r + k]).wait()
        return carry

    lax.fori_loop(0, tokens, issue, 0)
    lax.fori_loop(0, tokens, drain, 0)
    w = r_ref[...]
    o_ref[0] = x_ref[0] + gate_ref[0] * (w[:, 2:3] * buf[0] + w[:, 3:4] * buf[1])


def _combine(dest, y, x, gate, route, *, tc, name):
    bx, t, d = x.shape
    per = t // tc
    return pl.pallas_call(
        functools.partial(_combine_kernel, tokens=tc),
        grid=(bx * per,),
        in_specs=[pl.BlockSpec((1, 1, 2 * tc), lambda i: (i, 0, 0), memory_space=pltpu.SMEM),
                  pl.BlockSpec(memory_space=pl.ANY),
                  pl.BlockSpec((1, tc, d), lambda i: (i // per, i % per, 0)),
                  pl.BlockSpec((1, 1, d), lambda i: (i // per, 0, 0)),
                  pl.BlockSpec((tc, LANES), lambda i: (i, 0))],
        out_specs=pl.BlockSpec((1, tc, d), lambda i: (i // per, i % per, 0)),
        out_shape=jax.ShapeDtypeStruct(x.shape, F32),
        scratch_shapes=[pltpu.VMEM((2, tc, d), F32), pltpu.SemaphoreType.DMA(())],
        compiler_params=_params(1),
        name=name,
    )(dest.reshape(bx * per, 1, 2 * tc), y, x, gate, route)


def _expert_slots(e_flat, tm):
    n_assign = e_flat.shape[0]
    onehot = (e_flat[:, None] == jnp.arange(N_EXPERTS, dtype=jnp.int32)[None, :]).astype(jnp.int32)
    csum = jnp.cumsum(onehot, axis=0)
    counts = csum[-1]
    padded = ((counts + tm - 1) // tm) * tm
    ends = jnp.cumsum(padded)
    dest = jnp.sum(onehot * (csum - 1 + (ends - padded)[None, :]), axis=1)
    n_tiles = n_assign // tm + N_EXPERTS
    tile_expert = jnp.minimum(jnp.searchsorted(ends // tm, jnp.arange(n_tiles, dtype=jnp.int32), side="right"),
                              N_EXPERTS - 1).astype(jnp.int32)
    return dest.astype(jnp.int32), tile_expert, (ends[-1:] // tm).astype(jnp.int32), n_tiles


def _rope_tables(length):
    pairs = HEAD_DIM // 4
    pos = jnp.arange(length, dtype=jnp.int32)
    row = (pos // GRID_W).astype(F32)
    col = (pos % GRID_W).astype(F32)
    freqs = ROPE_THETA ** (-jnp.arange(pairs, dtype=F32) / pairs)
    ang = jnp.concatenate([row[:, None] * freqs, col[:, None] * freqs], axis=-1)
    cos = jnp.tile(jnp.cos(ang), (1, 4))
    sin = jnp.tile(jnp.concatenate([-jnp.sin(ang), jnp.sin(ang)], axis=-1), (1, 2))
    return cos, sin


def kernel(x, c, ctx, c_ctx, w_ada, b_ada, w_in, a_qnorm, a_knorm, b_qnorm, b_knorm, c_qnorm, c_knorm, lambda_q1, lambda_k1, lambda_q2, lambda_k2, b_subln, c_sink, w_branch_a, w_branch_b, w_branch_c, w_out, w_router_group, w_router_expert, w_exp_gate, w_exp_up, w_exp_down):
    bsz, length, d = x.shape
    c_len = ctx.shape[1]
    depth = w_ada.shape[0]
    assert d == D_MODEL and bsz + 1 <= MOD_ROWS and length % 512 == 0 and c_len == 256

    cos, sin = _rope_tables(length)
    no_rope = jnp.zeros((c_len, LANES), F32)
    head_ids = jnp.arange(NORM_CHUNK, dtype=jnp.int32) // HEAD_DIM
    ones = (head_ids[:, None] == head_ids[None, :]).astype(BF16)
    c_rows = jnp.concatenate([c, c_ctx[None, :], jnp.zeros((MOD_ROWS - bsz - 1, d), F32)], axis=0)
    mod_all = _ada(c_rows, w_ada, b_ada.reshape(depth, 1, -1))
    zero_sink = jnp.zeros((8,), F32)

    tq_a, tq_b, tk = 128, 256, 512
    n_lat = length // tk
    tm_moe = 256

    for l in range(depth):
        need_ctx = l < depth - 1
        mod = lambda k: mod_all[l, :bsz, k * d:(k + 1) * d].reshape(bsz, 1, d)
        cmod = lambda k: jnp.broadcast_to(mod_all[l, bsz, k * d:(k + 1) * d], (bsz, 1, d))
        lam_init = 0.8 - 0.6 * math.exp(-0.3 * l)
        lam = (jnp.exp(jnp.sum(lambda_q1[l] * lambda_k1[l])) - jnp.exp(jnp.sum(lambda_q2[l] * lambda_k2[l]))
               + lam_init).reshape(1).astype(F32)

        w = w_in[l]
        wqk = jnp.concatenate([w[:, 0:1536], w[:, 4864:5376], w[:, 4608:4736], w[:, 5888:6016]], axis=1).astype(BF16)
        wg = w[:, 1536:4608].astype(BF16)
        wv = jnp.concatenate([w[:, 5376:5888], w[:, 4736:4864], w[:, 6016:6144]], axis=1).astype(BF16)
        q_scale = HEAD_DIM ** -0.5
        gain = jnp.concatenate([jnp.tile(a_qnorm[l], 8) * q_scale, jnp.tile(b_qnorm[l], 8) * q_scale,
                                jnp.tile(c_qnorm[l], 8) * q_scale, jnp.tile(b_knorm[l], 8),
                                jnp.tile(a_knorm[l], 2), jnp.tile(c_knorm[l], 2)]).reshape(1, QK_COLS)
        subln = b_subln[l].reshape(1, LANES)
        sink = c_sink[l]
        merge_w = (w_branch_a[l].astype(BF16), w_branch_b[l].astype(BF16), w_branch_c[l].astype(BF16),
                   w_out[l].astype(BF16),
                   jnp.concatenate([w_router_group[l], w_router_expert[l],
                                    jnp.zeros((d, LANES - N_GROUPS - N_EXPERTS), F32)], axis=1))

        qk, gates, v = _inproj(x, mod(0), mod(1), wqk, wg, wv, gain, ones, cos, sin, use_rope=True, tm=512)
        cqk, cgates, cv = _inproj(ctx, cmod(0), cmod(1), wqk, wg, wv, gain, ones, no_rope, no_rope,
                                  use_rope=False, tm=c_len)
        ya = _gqa(zero_sink, qk, qk, v, cqk, cv, q_blk=0, k_blk=16, v_blk=4, tq=tq_a, tk=tk, n_lat=n_lat,
                  use_sink=False, name="attn_gqa")
        yb = _diff(lam, qk, qk, v, cqk, cv, subln, tq=tq_b, tk=tk, n_lat=n_lat, out_scale=1.0 - lam_init,
                   name="attn_diff")
        yc = _window(sink, qk, qk, v, cqk, cv, tq=128)
        x, h2, route = _merge(ya, yb, yc, gates, x, mod(2), mod(3), mod(4), *merge_w, tm=512, name="merge")
        tokens = [h2.reshape(bsz * length, d)]
        routes = [route.reshape(bsz * length, LANES)]
        if need_ctx:
            cya = _gqa(zero_sink, cqk, cqk, cv, cqk, cv, q_blk=0, k_blk=16, v_blk=4, tq=128, tk=tk, n_lat=0,
                       use_sink=False, name="ctx_gqa")
            cyb = _diff(lam, cqk, cqk, cv, cqk, cv, subln, tq=c_len, tk=tk, n_lat=0, out_scale=1.0 - lam_init,
                        name="ctx_diff")
            cyc = _gqa(sink, cqk, cqk, cv, cqk, cv, q_blk=2, k_blk=17, v_blk=5, tq=128, tk=tk, n_lat=0,
                       use_sink=True, name="ctx_sink")
            ctx, hc2, croute = _merge(cya, cyb, cyc, cgates, ctx, cmod(2), cmod(3), cmod(4), *merge_w, tm=c_len,
                                      name="ctx_merge")
            tokens.append(hc2.reshape(bsz * c_len, d))
            routes.append(croute.reshape(bsz * c_len, LANES))

        e_flat = jnp.concatenate([r[:, 0:2].reshape(-1) for r in routes]).astype(jnp.int32)
        dest, tile_expert, n_used, n_tiles = _expert_slots(e_flat, tm_moe)
        xs = jnp.zeros((n_tiles * tm_moe, d), F32)
        dests, lo = [], 0
        for t in tokens:
            dests.append(lax.slice(dest, (lo,), (lo + 2 * t.shape[0],)))
            lo += 2 * t.shape[0]
            xs = _dispatch(dests[-1], t, xs, td=256)
        y = _experts(tile_expert, n_used, xs, w_exp_gate[l].astype(BF16), w_exp_up[l].astype(BF16),
                     w_exp_down[l].astype(BF16), tm=tm_moe)
        x = _combine(dests[0], y, x, mod(5), routes[0], tc=256, name="moe_combine")
        if need_ctx:
            ctx = _combine(dests[1], y, ctx, cmod(5), routes[1], tc=256, name="ctx_combine")
    return x
```

```python
import functools
import math

import jax
import jax.numpy as jnp
from jax import lax
from jax.experimental import pallas as pl
from jax.experimental.pallas import tpu as pltpu

F32 = jnp.float32
BF16 = jnp.bfloat16

D_MODEL = 1024
HEAD_DIM = 64
GRID_W = 64
ROPE_THETA = 10000.0
WINDOW = 128
N_GROUPS = 4
EXPERTS_PER_GROUP = 8
N_EXPERTS = N_GROUPS * EXPERTS_PER_GROUP
EXPERT_HIDDEN = D_MODEL // 2
N_MOD = 6
EPS = 1e-6
NEG = -1e30
LOG2E = math.log2(math.e)
LANES = 128
MOD_ROWS = 16

QK_COLS = 2304
G_COLS = 3 * D_MODEL
V_COLS = 768
NORM_CHUNK = 256
SUB = 256

VMEM_LIMIT = 48 * 1024 * 1024


def _params(n_axes, vmem=VMEM_LIMIT):
    return pltpu.CompilerParams(dimension_semantics=("arbitrary",) * n_axes, vmem_limit_bytes=vmem)


def _resident(shape):
    return pl.BlockSpec(shape, lambda *_: (0,) * len(shape), pipeline_mode=pl.Buffered(1))


def _sigmoid(v):
    return 1.0 / (1.0 + jnp.exp(-v))


def _rms_modulate(v, shift, scale):
    v = v * lax.rsqrt(jnp.mean(v * v, axis=-1, keepdims=True) + EPS)
    return v * (1.0 + scale) + shift


def _ada_kernel(c_ref, w_ref, b_ref, o_ref):
    c = c_ref[...]
    o_ref[0] = jnp.dot(c * _sigmoid(c), w_ref[0], preferred_element_type=F32,
                       precision=lax.Precision.HIGHEST) + b_ref[0]


def _ada(c_rows, w_ada, b_ada):
    depth, d, n = w_ada.shape
    tn = 1536
    return pl.pallas_call(
        _ada_kernel,
        grid=(depth, n // tn),
        in_specs=[pl.BlockSpec((MOD_ROWS, d), lambda l, j: (0, 0)),
                  pl.BlockSpec((1, d, tn), lambda l, j: (l, 0, j)),
                  pl.BlockSpec((1, 1, tn), lambda l, j: (l, 0, j))],
        out_specs=pl.BlockSpec((1, MOD_ROWS, tn), lambda l, j: (l, 0, j)),
        out_shape=jax.ShapeDtypeStruct((depth, MOD_ROWS, n), F32),
        compiler_params=_params(2),
        name="ada",
    )(c_rows, w_ada, b_ada)


def _rope_partner(z):
    lane = lax.broadcasted_iota(jnp.int32, z.shape, 1)
    return jnp.where((lane & 32) == 0, pltpu.roll(z, 96, 1), pltpu.roll(z, 32, 1))


def _inproj_kernel(x_ref, shift_ref, scale_ref, wqk_ref, wg_ref, wv_ref, gain_ref, ones_ref, cos_ref, sin_ref,
                   oqk_ref, og_ref, ov_ref, *, use_rope):
    hb = _rms_modulate(x_ref[0], shift_ref[0], scale_ref[0]).astype(BF16)
    for c in range(G_COLS // 512):
        cols = slice(c * 512, (c + 1) * 512)
        og_ref[0, :, cols] = _sigmoid(jnp.dot(hb, wg_ref[:, cols], preferred_element_type=F32)).astype(BF16)
    ov_ref[0] = jnp.dot(hb, wv_ref[...], preferred_element_type=F32).astype(BF16)
    for c in range(QK_COLS // NORM_CHUNK):
        cols = slice(c * NORM_CHUNK, (c + 1) * NORM_CHUNK)
        y = jnp.dot(hb, wqk_ref[:, cols], preferred_element_type=F32)
        ss = jnp.dot((y * y).astype(BF16), ones_ref[...], preferred_element_type=F32)
        yn = y * lax.rsqrt(ss * (1.0 / HEAD_DIM) + EPS) * gain_ref[:, cols]
        if use_rope:
            halves = []
            for k in range(NORM_CHUNK // LANES):
                z = yn[:, k * LANES:(k + 1) * LANES]
                halves.append(z * cos_ref[...] + _rope_partner(z) * sin_ref[...])
            yn = jnp.concatenate(halves, axis=1)
        oqk_ref[0, :, cols] = yn.astype(BF16)


def _inproj(x, shift, scale, wqk, wg, wv, gain, ones, cos, sin, *, use_rope, tm):
    bx, t, d = x.shape
    row = lambda b, i: (b, 0, 0)
    tile = lambda b, i: (b, i, 0)
    return pl.pallas_call(
        functools.partial(_inproj_kernel, use_rope=use_rope),
        grid=(bx, t // tm),
        in_specs=[pl.BlockSpec((1, tm, d), tile),
                  pl.BlockSpec((1, 1, d), row), pl.BlockSpec((1, 1, d), row),
                  _resident((d, QK_COLS)), _resident((d, G_COLS)), _resident((d, V_COLS)),
                  _resident((1, QK_COLS)), _resident((NORM_CHUNK, NORM_CHUNK)),
                  pl.BlockSpec((tm, LANES), lambda b, i: (i, 0)), pl.BlockSpec((tm, LANES), lambda b, i: (i, 0))],
        out_specs=[pl.BlockSpec((1, tm, QK_COLS), tile), pl.BlockSpec((1, tm, G_COLS), tile),
                   pl.BlockSpec((1, tm, V_COLS), tile)],
        out_shape=[jax.ShapeDtypeStruct((bx, t, QK_COLS), BF16), jax.ShapeDtypeStruct((bx, t, G_COLS), BF16),
                   jax.ShapeDtypeStruct((bx, t, V_COLS), BF16)],
        compiler_params=_params(2),
        name="inproj_rope" if use_rope else "inproj_ctx",
    )(x, shift, scale, wqk, wg, wv, gain, ones, cos, sin)


def _qk(qs, k):
    return lax.dot_general(qs, k, (((1,), (1,)), ((), ())), preferred_element_type=F32)


def _online_softmax(streams, state, kl_ref, vl_ref, kc_ref, vc_ref, n_lat, tk):
    n = len(streams)
    state = [state[4 * i:4 * i + 4] for i in range(n)]
    for (qs, _, m0, l0), (qs_ref, m_ref, l_ref, acc_ref) in zip(streams, state):
        lane = lax.broadcasted_iota(jnp.int32, m_ref.shape, 1)
        qs_ref[...] = qs
        m_ref[...] = jnp.broadcast_to(m0, m_ref.shape)
        l_ref[...] = jnp.where(lane == 0, l0, 0.0)
        acc_ref[...] = jnp.zeros(acc_ref.shape, F32)

    def step(i, key_block, value_block, width):
        qs_ref, m_ref, l_ref, acc_ref = state[i]
        qs = qs_ref[...]
        m = m_ref[...]
        subs = [_qk(qs, key_block(c)) for c in range(width // SUB)]
        top = subs[0]
        for s in subs[1:]:
            top = jnp.maximum(top, s)
        top = functools.reduce(jnp.maximum, [top[:, k * LANES:(k + 1) * LANES] for k in range(SUB // LANES)])
        m_new = jnp.maximum(m, jnp.max(top, axis=-1, keepdims=True))
        alpha = jnp.exp2(m - m_new)
        m_wide = jnp.concatenate([m_new] * (SUB // LANES), axis=1)
        l = alpha * l_ref[...]
        acc = alpha * acc_ref[...]
        for c, s in enumerate(subs):
            p = jnp.exp2(s - m_wide)
            for k in range(SUB // LANES):
                l = l + p[:, k * LANES:(k + 1) * LANES]
            acc = acc + jnp.dot(p.astype(BF16), value_block(c), preferred_element_type=F32)
        m_ref[...] = m_new
        l_ref[...] = l
        acc_ref[...] = acc

    if n_lat:
        def body(j, carry):
            for i, (_, lanes, _, _) in enumerate(streams):
                rows = lambda c: pl.ds(pl.multiple_of(j * tk + c * SUB, SUB), SUB)
                step(i, lambda c: kl_ref[0, rows(c), lanes], lambda c: vl_ref[0, rows(c), lanes], tk)
            return carry
        lax.fori_loop(0, n_lat, body, 0)
    c_len = kc_ref.shape[1]
    for i, (_, lanes, _, _) in enumerate(streams):
        step(i, lambda c: kc_ref[0, c * SUB:(c + 1) * SUB, lanes], lambda c: vc_ref[0, c * SUB:(c + 1) * SUB, lanes],
             c_len)
    return [acc_ref[...] / jnp.sum(l_ref[...], axis=-1, keepdims=True) for _, _, l_ref, acc_ref in state]


def _softmax_state(n_streams, rows):
    return [pltpu.VMEM((rows, LANES), BF16), pltpu.VMEM((rows, LANES), F32), pltpu.VMEM((rows, LANES), F32),
            pltpu.VMEM((rows, LANES), F32)] * n_streams


def _stack_pair(q_ref, c, tq):
    g = c // 2
    lane = lax.broadcasted_iota(jnp.int32, (tq, LANES), 1)
    keep = (lane >= HEAD_DIM) if g else (lane < HEAD_DIM)
    z = q_ref[0, :, c * LANES:(c + 1) * LANES].astype(F32)
    swapped = pltpu.roll(z, HEAD_DIM, 1)
    first, second = (z, swapped) if g == 0 else (swapped, z)
    return jnp.concatenate([jnp.where(keep, first, 0.0).astype(BF16), jnp.where(keep, second, 0.0).astype(BF16)], axis=0)


def _unstack_pair(o, c, tq):
    lane = lax.broadcasted_iota(jnp.int32, (tq, LANES), 1)
    a, b = o[:tq], o[tq:]
    if c // 2 == 0:
        b = pltpu.roll(b, HEAD_DIM, 1)
    else:
        a = pltpu.roll(a, HEAD_DIM, 1)
    return jnp.where(lane < HEAD_DIM, a, b)


def _sink_rows(sink_ref, heads, tq):
    return jnp.concatenate([jnp.full((tq, 1), sink_ref[h], F32) for h in heads], axis=0)


def _gqa_kernel(sink_ref, q_ref, kl_ref, vl_ref, kc_ref, vc_ref, o_ref, *state, tq, tk, n_lat, use_sink):
    streams = []
    for c in range(4):
        if use_sink:
            m0, l0 = _sink_rows(sink_ref, (2 * c, 2 * c + 1), tq), jnp.ones((2 * tq, 1), F32)
        else:
            m0, l0 = jnp.full((2 * tq, 1), NEG, F32), jnp.zeros((2 * tq, 1), F32)
        streams.append((_stack_pair(q_ref, c, tq), slice(0, LANES), m0, l0))
    outs = _online_softmax(streams, state, kl_ref, vl_ref, kc_ref, vc_ref, n_lat, tk)
    for c, o in enumerate(outs):
        o_ref[0, :, c * LANES:(c + 1) * LANES] = _unstack_pair(o, c, tq).astype(BF16)


def _diff_kernel(lam_ref, q_ref, kl_ref, vl_ref, kc_ref, vc_ref, gain_ref, o_ref, *state, tq, tk, n_lat, out_scale):
    lane = lax.broadcasted_iota(jnp.int32, (tq, LANES), 1)
    streams = []
    for n in range(4):
        lanes = slice(n * LANES, (n + 1) * LANES)
        z = q_ref[0, :, lanes]
        zero = jnp.zeros_like(z)
        qs = jnp.concatenate([jnp.where(lane < HEAD_DIM, z, zero), jnp.where(lane >= HEAD_DIM, z, zero)], axis=0)
        streams.append((qs, lanes, jnp.full((2 * tq, 1), NEG, F32), jnp.zeros((2 * tq, 1), F32)))
    outs = _online_softmax(streams, state, kl_ref, vl_ref, kc_ref, vc_ref, n_lat, tk)
    for n, o in enumerate(outs):
        d = o[:tq] - lam_ref[0] * o[tq:]
        d = d * lax.rsqrt(jnp.mean(d * d, axis=-1, keepdims=True) + EPS) * gain_ref[...] * out_scale
        o_ref[0, :, n * LANES:(n + 1) * LANES] = d.astype(BF16)


def _window_kernel(sink_ref, q_ref, kl_ref, vl_ref, kc_ref, vc_ref, o_ref, *, tq, length):
    i = pl.program_id(1)
    span = tq + 2 * WINDOW
    start = pl.multiple_of(jnp.clip(i * tq - WINDOW, 0, length - span), LANES)
    kw = kl_ref[0, pl.ds(start, span), :]
    vw = vl_ref[0, pl.ds(start, span), :]
    kc = kc_ref[0]
    vc = vc_ref[0]
    rows = 2 * tq
    rel = (start - i * tq + lax.broadcasted_iota(jnp.int32, (rows, span), 1)
           - (lax.broadcasted_iota(jnp.int32, (rows, span), 0) & (tq - 1)))
    valid = jnp.abs(rel) <= WINDOW
    for c in range(4):
        qs = _stack_pair(q_ref, c, tq)
        sw = jnp.where(valid, _qk(qs, kw), NEG)
        sc = _qk(qs, kc)
        snk = _sink_rows(sink_ref, (2 * c, 2 * c + 1), tq)
        m = jnp.maximum(jnp.maximum(jnp.max(sw, axis=-1, keepdims=True), jnp.max(sc, axis=-1, keepdims=True)), snk)
        pw = jnp.exp2(sw - m)
        pc = jnp.exp2(sc - m)
        l = jnp.sum(pw, axis=-1, keepdims=True) + jnp.sum(pc, axis=-1, keepdims=True) + jnp.exp2(snk - m)
        o = (jnp.dot(pw.astype(BF16), vw, preferred_element_type=F32)
             + jnp.dot(pc.astype(BF16), vc, preferred_element_type=F32)) / l
        o_ref[0, :, c * LANES:(c + 1) * LANES] = _unstack_pair(o, c, tq).astype(BF16)


_SMEM = pl.BlockSpec(memory_space=pltpu.SMEM)


def _kv_specs(klat, kctx, k_blk, v_blk, width):
    s_lat, s_ctx = klat.shape[1], kctx.shape[1]
    return [pl.BlockSpec((1, s_lat, width), lambda b, i: (b, 0, k_blk)),
            pl.BlockSpec((1, s_lat, width), lambda b, i: (b, 0, v_blk)),
            pl.BlockSpec((1, s_ctx, width), lambda b, i: (b, 0, k_blk)),
            pl.BlockSpec((1, s_ctx, width), lambda b, i: (b, 0, v_blk))]


def _gqa(sink, q, klat, vlat, kctx, vctx, *, q_blk, k_blk, v_blk, tq, tk, n_lat, use_sink, name):
    bx, t, _ = q.shape
    return pl.pallas_call(
        functools.partial(_gqa_kernel, tq=tq, tk=tk, n_lat=n_lat, use_sink=use_sink),
        grid=(bx, t // tq),
        in_specs=[_SMEM, pl.BlockSpec((1, tq, 512), lambda b, i: (b, i, q_blk))]
        + _kv_specs(klat, kctx, k_blk, v_blk, LANES),
        out_specs=pl.BlockSpec((1, tq, 512), lambda b, i: (b, i, 0)),
        out_shape=jax.ShapeDtypeStruct((bx, t, 512), BF16),
        scratch_shapes=_softmax_state(4, 2 * tq),
        compiler_params=_params(2),
        name=name,
    )(sink, q, klat, vlat, kctx, vctx)


def _diff(lam, q, klat, vlat, kctx, vctx, gain, *, tq, tk, n_lat, out_scale, name):
    bx, t, _ = q.shape
    return pl.pallas_call(
        functools.partial(_diff_kernel, tq=tq, tk=tk, n_lat=n_lat, out_scale=out_scale),
        grid=(bx, t // tq),
        in_specs=[_SMEM, pl.BlockSpec((1, tq, 512), lambda b, i: (b, i, 1))]
        + _kv_specs(klat, kctx, 3, 0, 512) + [pl.BlockSpec((1, LANES), lambda b, i: (0, 0))],
        out_specs=pl.BlockSpec((1, tq, 512), lambda b, i: (b, i, 0)),
        out_shape=jax.ShapeDtypeStruct((bx, t, 512), BF16),
        scratch_shapes=_softmax_state(4, 2 * tq),
        compiler_params=_params(2),
        name=name,
    )(lam, q, klat, vlat, kctx, vctx, gain)


def _window(sink, q, klat, vlat, kctx, vctx, *, tq):
    bx, t, _ = q.shape
    return pl.pallas_call(
        functools.partial(_window_kernel, tq=tq, length=t),
        grid=(bx, t // tq),
        in_specs=[_SMEM, pl.BlockSpec((1, tq, 512), lambda b, i: (b, i, 2))]
        + _kv_specs(klat, kctx, 17, 5, LANES),
        out_specs=pl.BlockSpec((1, tq, 512), lambda b, i: (b, i, 0)),
        out_shape=jax.ShapeDtypeStruct((bx, t, 512), BF16),
        compiler_params=_params(2),
        name="attn_window",
    )(sink, q, klat, vlat, kctx, vctx)


def _route(logits):
    lane = lax.broadcasted_iota(jnp.int32, logits.shape, 1).astype(F32)
    first = lambda hit: jnp.min(jnp.where(hit, lane, float(LANES)), axis=-1, keepdims=True)
    gl = jnp.where(lane < N_GROUPS, logits, NEG)
    gmax = jnp.max(gl, axis=-1, keepdims=True)
    gidx = first(gl == gmax)
    gw = 1.0 / jnp.sum(jnp.exp(gl - gmax), axis=-1, keepdims=True)
    lo = N_GROUPS + gidx * EXPERTS_PER_GROUP
    el = jnp.where((lane >= lo) & (lane < lo + EXPERTS_PER_GROUP), logits, NEG)
    v1 = jnp.max(el, axis=-1, keepdims=True)
    i1 = first(el == v1)
    el = jnp.where(lane == i1, NEG, el)
    v2 = jnp.max(el, axis=-1, keepdims=True)
    i2 = first(el == v2)
    e = jnp.exp(v2 - v1)
    w1 = gw / (1.0 + e)
    w2 = gw * e / (1.0 + e)
    return jnp.where(lane == 0, i1 - N_GROUPS,
                     jnp.where(lane == 1, i2 - N_GROUPS, jnp.where(lane == 2, w1, jnp.where(lane == 3, w2, 0.0))))


def _merge_kernel(ya_ref, yb_ref, yc_ref, g_ref, x_ref, gate_ref, shift_ref, scale_ref,
                  wa_ref, wb_ref, wc_ref, wo_ref, wr_ref, xo_ref, h_ref, r_ref):
    d = D_MODEL
    m = None
    for k, (y_ref, w_ref) in enumerate(((ya_ref, wa_ref), (yb_ref, wb_ref), (yc_ref, wc_ref))):
        t = g_ref[0, :, k * d:(k + 1) * d].astype(F32) * jnp.dot(y_ref[0], w_ref[...], preferred_element_type=F32)
        m = t if m is None else m + t
    xn = x_ref[0] + gate_ref[0] * jnp.dot(m.astype(BF16), wo_ref[...], preferred_element_type=F32)
    xo_ref[0] = xn
    h2 = _rms_modulate(xn, shift_ref[0], scale_ref[0])
    h_ref[0] = h2
    r_ref[0] = _route(jnp.dot(h2, wr_ref[...], preferred_element_type=F32, precision=lax.Precision.HIGHEST))


def _merge(ya, yb, yc, g, x, gate, shift, scale, wa, wb, wc, wo, wr, *, tm, name):
    bx, t, d = x.shape
    row = lambda b, i: (b, 0, 0)
    tile = lambda b, i: (b, i, 0)
    return pl.pallas_call(
        _merge_kernel,
        grid=(bx, t // tm),
        in_specs=[pl.BlockSpec((1, tm, 512), tile)] * 3
        + [pl.BlockSpec((1, tm, G_COLS), tile), pl.BlockSpec((1, tm, d), tile)]
        + [pl.BlockSpec((1, 1, d), row)] * 3
        + [_resident((512, d))] * 3 + [_resident((d, d)), _resident((d, LANES))],
        out_specs=[pl.BlockSpec((1, tm, d), tile), pl.BlockSpec((1, tm, d), tile), pl.BlockSpec((1, tm, LANES), tile)],
        out_shape=[jax.ShapeDtypeStruct((bx, t, d), F32), jax.ShapeDtypeStruct((bx, t, d), F32),
                   jax.ShapeDtypeStruct((bx, t, LANES), F32)],
        compiler_params=_params(2),
        name=name,
    )(ya, yb, yc, g, x, gate, shift, scale, wa, wb, wc, wo, wr)


def _dispatch_kernel(dest_ref, h_ref, xs_in_ref, xs_ref, sem, *, tokens):
    del xs_in_ref

    def row_copy(r, d):
        return pltpu.make_async_copy(h_ref.at[pl.ds(r, 1), :], xs_ref.at[pl.ds(d, 1), :], sem)

    def issue(r, carry):
        for k in range(2):
            row_copy(r, dest_ref[0, 0, 2 * r + k]).start()
        return carry

    lax.fori_loop(0, tokens, issue, 0)
    for _ in range(2):
        pltpu.make_async_copy(h_ref, xs_ref.at[pl.ds(0, tokens), :], sem).wait()


def _dispatch(dest, h, xs, *, td):
    n, d = h.shape
    return pl.pallas_call(
        functools.partial(_dispatch_kernel, tokens=td),
        grid=(n // td,),
        in_specs=[pl.BlockSpec((1, 1, 2 * td), lambda i: (i, 0, 0), memory_space=pltpu.SMEM),
                  pl.BlockSpec((td, d), lambda i: (i, 0)),
                  pl.BlockSpec(memory_space=pl.ANY)],
        out_specs=pl.BlockSpec(memory_space=pl.ANY),
        out_shape=jax.ShapeDtypeStruct(xs.shape, xs.dtype),
        scratch_shapes=[pltpu.SemaphoreType.DMA(())],
        input_output_aliases={2: 0},
        compiler_params=_params(1),
        name="moe_dispatch",
    )(dest.reshape(n // td, 1, 2 * td), h, xs)


def _experts_kernel(te_ref, used_ref, xs_ref, wg_ref, wu_ref, wd_ref, y_ref):
    del te_ref
    t = pl.program_id(0)

    @pl.when(t < used_ref[0])
    def _():
        xb = xs_ref[...].astype(BF16)
        a = jnp.dot(xb, wg_ref[0], preferred_element_type=F32)
        u = jnp.dot(xb, wu_ref[0], preferred_element_type=F32)
        y_ref[...] = jnp.dot((a * _sigmoid(a) * u).astype(BF16), wd_ref[0], preferred_element_type=F32)

    @pl.when(t >= used_ref[0])
    def _():
        y_ref[...] = jnp.zeros_like(y_ref)


def _experts(tile_expert, n_used, xs, wg, wu, wd, *, tm):
    p, d = xs.shape
    hid = wg.shape[2]
    return pl.pallas_call(
        _experts_kernel,
        grid_spec=pltpu.PrefetchScalarGridSpec(
            num_scalar_prefetch=2,
            grid=(p // tm,),
            in_specs=[pl.BlockSpec((tm, d), lambda t, te, nu: (t, 0)),
                      pl.BlockSpec((1, d, hid), lambda t, te, nu: (te[t], 0, 0)),
                      pl.BlockSpec((1, d, hid), lambda t, te, nu: (te[t], 0, 0)),
                      pl.BlockSpec((1, hid, d), lambda t, te, nu: (te[t], 0, 0))],
            out_specs=pl.BlockSpec((tm, d), lambda t, te, nu: (t, 0))),
        out_shape=jax.ShapeDtypeStruct((p, d), F32),
        compiler_params=_params(1),
        name="moe_experts",
    )(tile_expert, n_used, xs, wg, wu, wd)


def _combine_kernel(dest_ref, y_ref, x_ref, gate_ref, r_ref, o_ref, buf, sem, *, tokens):
    def row_copy(r, k, d):
        return pltpu.make_async_copy(y_ref.at[pl.ds(d, 1), :], buf.at[k, pl.ds(r, 1), :], sem)

    def issue(r, carry):
        for k in range(2):
            row_copy(r, k, dest_ref[0, 0, 2 * r + k]).start()
        return carry

    lax.fori_loop(0, tokens, issue, 0)
    for k in range(2):
        pltpu.make_async_copy(y_ref.at[pl.ds(0, tokens), :], buf.at[k], sem).wait()
    w = r_ref[...]
    o_ref[0] = x_ref[0] + gate_ref[0] * (w[:, 2:3] * buf[0] + w[:, 3:4] * buf[1])


def _combine(dest, y, x, gate, route, *, tc, name):
    bx, t, d = x.shape
    per = t // tc
    return pl.pallas_call(
        functools.partial(_combine_kernel, tokens=tc),
        grid=(bx * per,),
        in_specs=[pl.BlockSpec((1, 1, 2 * tc), lambda i: (i, 0, 0), memory_space=pltpu.SMEM),
                  pl.BlockSpec(memory_space=pl.ANY),
                  pl.BlockSpec((1, tc, d), lambda i: (i // per, i % per, 0)),
                  pl.BlockSpec((1, 1, d), lambda i: (i // per, 0, 0)),
                  pl.BlockSpec((tc, LANES), lambda i: (i, 0))],
        out_specs=pl.BlockSpec((1, tc, d), lambda i: (i // per, i % per, 0)),
        out_shape=jax.ShapeDtypeStruct(x.shape, F32),
        scratch_shapes=[pltpu.VMEM((2, tc, d), F32), pltpu.SemaphoreType.DMA(())],
        compiler_params=_params(1),
        name=name,
    )(dest.reshape(bx * per, 1, 2 * tc), y, x, gate, route)


def _expert_slots(e_flat, tm):
    n_assign = e_flat.shape[0]
    onehot = (e_flat[:, None] == jnp.arange(N_EXPERTS, dtype=jnp.int32)[None, :]).astype(jnp.int32)
    csum = jnp.cumsum(onehot, axis=0)
    counts = csum[-1]
    padded = ((counts + tm - 1) // tm) * tm
    ends = jnp.cumsum(padded)
    dest = jnp.sum(onehot * (csum - 1 + (ends - padded)[None, :]), axis=1)
    n_tiles = n_assign // tm + N_EXPERTS
    tile_expert = jnp.minimum(jnp.searchsorted(ends // tm, jnp.arange(n_tiles, dtype=jnp.int32), side="right"),
                              N_EXPERTS - 1).astype(jnp.int32)
    return dest.astype(jnp.int32), tile_expert, (ends[-1:] // tm).astype(jnp.int32), n_tiles


def _rope_tables(length):
    pairs = HEAD_DIM // 4
    pos = jnp.arange(length, dtype=jnp.int32)
    row = (pos // GRID_W).astype(F32)
    col = (pos % GRID_W).astype(F32)
    freqs = ROPE_THETA ** (-jnp.arange(pairs, dtype=F32) / pairs)
    ang = jnp.concatenate([row[:, None] * freqs, col[:, None] * freqs], axis=-1)
    cos = jnp.tile(jnp.cos(ang), (1, 4))
    sin = jnp.tile(jnp.concatenate([-jnp.sin(ang), jnp.sin(ang)], axis=-1), (1, 2))
    return cos, sin


def kernel(x, c, ctx, c_ctx, w_ada, b_ada, w_in, a_qnorm, a_knorm, b_qnorm, b_knorm, c_qnorm, c_knorm, lambda_q1, lambda_k1, lambda_q2, lambda_k2, b_subln, c_sink, w_branch_a, w_branch_b, w_branch_c, w_out, w_router_group, w_router_expert, w_exp_gate, w_exp_up, w_exp_down):
    bsz, length, d = x.shape
    c_len = ctx.shape[1]
    depth = w_ada.shape[0]
    assert d == D_MODEL and bsz + 1 <= MOD_ROWS and length % 512 == 0 and c_len == 256

    cos, sin = _rope_tables(length)
    no_rope = jnp.zeros((c_len, LANES), F32)
    head_ids = jnp.arange(NORM_CHUNK, dtype=jnp.int32) // HEAD_DIM
    ones = (head_ids[:, None] == head_ids[None, :]).astype(BF16)
    c_rows = jnp.concatenate([c, c_ctx[None, :], jnp.zeros((MOD_ROWS - bsz - 1, d), F32)], axis=0)
    mod_all = _ada(c_rows, w_ada, b_ada.reshape(depth, 1, -1))
    zero_sink = jnp.zeros((8,), F32)

    tq_a, tq_b = 256, 256
    tk = 1024 if length % 1024 == 0 else 512
    n_lat = length // tk
    tm_moe = 256

    for l in range(depth):
        need_ctx = l < depth - 1
        mod = lambda k: mod_all[l, :bsz, k * d:(k + 1) * d].reshape(bsz, 1, d)
        cmod = lambda k: jnp.broadcast_to(mod_all[l, bsz, k * d:(k + 1) * d], (bsz, 1, d))
        lam_init = 0.8 - 0.6 * math.exp(-0.3 * l)
        lam = (jnp.exp(jnp.sum(lambda_q1[l] * lambda_k1[l])) - jnp.exp(jnp.sum(lambda_q2[l] * lambda_k2[l]))
               + lam_init).reshape(1).astype(F32)

        w = w_in[l]
        wqk = jnp.concatenate([w[:, 0:1536], w[:, 4864:5376], w[:, 4608:4736], w[:, 5888:6016]], axis=1).astype(BF16)
        wg = w[:, 1536:4608].astype(BF16)
        wv = jnp.concatenate([w[:, 5376:5888], w[:, 4736:4864], w[:, 6016:6144]], axis=1).astype(BF16)
        q_scale = HEAD_DIM ** -0.5 * LOG2E
        gain = jnp.concatenate([jnp.tile(a_qnorm[l], 8) * q_scale, jnp.tile(b_qnorm[l], 8) * q_scale,
                                jnp.tile(c_qnorm[l], 8) * q_scale, jnp.tile(b_knorm[l], 8),
                                jnp.tile(a_knorm[l], 2), jnp.tile(c_knorm[l], 2)]).reshape(1, QK_COLS)
        subln = b_subln[l].reshape(1, LANES)
        sink = c_sink[l] * LOG2E
        merge_w = (w_branch_a[l].astype(BF16), w_branch_b[l].astype(BF16), w_branch_c[l].astype(BF16),
                   w_out[l].astype(BF16),
                   jnp.concatenate([w_router_group[l], w_router_expert[l],
                                    jnp.zeros((d, LANES - N_GROUPS - N_EXPERTS), F32)], axis=1))

        qk, gates, v = _inproj(x, mod(0), mod(1), wqk, wg, wv, gain, ones, cos, sin, use_rope=True, tm=512)
        cqk, cgates, cv = _inproj(ctx, cmod(0), cmod(1), wqk, wg, wv, gain, ones, no_rope, no_rope,
                                  use_rope=False, tm=c_len)
        ya = _gqa(zero_sink, qk, qk, v, cqk, cv, q_blk=0, k_blk=16, v_blk=4, tq=tq_a, tk=tk, n_lat=n_lat,
                  use_sink=False, name="attn_gqa")
        yb = _diff(lam, qk, qk, v, cqk, cv, subln, tq=tq_b, tk=tk, n_lat=n_lat, out_scale=1.0 - lam_init,
                   name="attn_diff")
        yc = _window(sink, qk, qk, v, cqk, cv, tq=128)
        x, h2, route = _merge(ya, yb, yc, gates, x, mod(2), mod(3), mod(4), *merge_w, tm=512, name="merge")
        tokens = [h2.reshape(bsz * length, d)]
        routes = [route.reshape(bsz * length, LANES)]
        if need_ctx:
            cya = _gqa(zero_sink, cqk, cqk, cv, cqk, cv, q_blk=0, k_blk=16, v_blk=4, tq=128, tk=tk, n_lat=0,
                       use_sink=False, name="ctx_gqa")
            cyb = _diff(lam, cqk, cqk, cv, cqk, cv, subln, tq=c_len, tk=tk, n_lat=0, out_scale=1.0 - lam_init,
                        name="ctx_diff")
            cyc = _gqa(sink, cqk, cqk, cv, cqk, cv, q_blk=2, k_blk=17, v_blk=5, tq=128, tk=tk, n_lat=0,
                       use_sink=True, name="ctx_sink")
            ctx, hc2, croute = _merge(cya, cyb, cyc, cgates, ctx, cmod(2), cmod(3), cmod(4), *merge_w, tm=c_len,
                                      name="ctx_merge")
            tokens.append(hc2.reshape(bsz * c_len, d))
            routes.append(croute.reshape(bsz * c_len, LANES))

        e_flat = jnp.concatenate([r[:, 0:2].reshape(-1) for r in routes]).astype(jnp.int32)
        dest, tile_expert, n_used, n_tiles = _expert_slots(e_flat, tm_moe)
        xs = jnp.zeros((n_tiles * tm_moe, d), F32)
        dests, lo = [], 0
        for t in tokens:
            dests.append(lax.slice(dest, (lo,), (lo + 2 * t.shape[0],)))
            lo += 2 * t.shape[0]
            xs = _dispatch(dests[-1], t, xs, td=256)
        y = _experts(tile_expert, n_used, xs, w_exp_gate[l].astype(BF16), w_exp_up[l].astype(BF16),
                     w_exp_down[l].astype(BF16), tm=tm_moe)
        x = _combine(dests[0], y, x, mod(5), routes[0], tc=256, name="moe_combine")
        if need_ctx:
            ctx = _combine(dests[1], y, ctx, cmod(5), routes[1], tc=256, name="ctx_combine")
    return x
```

```python
import functools
import math

import jax
import jax.numpy as jnp
from jax import lax
from jax.experimental import pallas as pl
from jax.experimental.pallas import tpu as pltpu

F32 = jnp.float32
BF16 = jnp.bfloat16

D_MODEL = 1024
HEAD_DIM = 64
GRID_W = 64
ROPE_THETA = 10000.0
WINDOW = 128
N_GROUPS = 4
EXPERTS_PER_GROUP = 8
N_EXPERTS = N_GROUPS * EXPERTS_PER_GROUP
EXPERT_HIDDEN = D_MODEL // 2
N_MOD = 6
EPS = 1e-6
NEG = -1e30
LOG2E = math.log2(math.e)
LANES = 128
MOD_ROWS = 16

QK_COLS = 2304
G_COLS = 3 * D_MODEL
V_COLS = 768
NORM_CHUNK = 256
SUB = 256

VMEM_LIMIT = 48 * 1024 * 1024


def _params(n_axes, vmem=VMEM_LIMIT):
    return pltpu.CompilerParams(dimension_semantics=("arbitrary",) * n_axes, vmem_limit_bytes=vmem)


def _resident(shape):
    return pl.BlockSpec(shape, lambda *_: (0,) * len(shape), pipeline_mode=pl.Buffered(1))


def _sigmoid(v):
    return 1.0 / (1.0 + jnp.exp(-v))


def _rms_modulate(v, shift, scale):
    v = v * lax.rsqrt(jnp.mean(v * v, axis=-1, keepdims=True) + EPS)
    return v * (1.0 + scale) + shift


def _ada_kernel(c_ref, w_ref, b_ref, o_ref):
    c = c_ref[...]
    o_ref[0] = jnp.dot(c * _sigmoid(c), w_ref[0], preferred_element_type=F32,
                       precision=lax.Precision.HIGHEST) + b_ref[0]


def _ada(c_rows, w_ada, b_ada):
    depth, d, n = w_ada.shape
    tn = 1536
    return pl.pallas_call(
        _ada_kernel,
        grid=(depth, n // tn),
        in_specs=[pl.BlockSpec((MOD_ROWS, d), lambda l, j: (0, 0)),
                  pl.BlockSpec((1, d, tn), lambda l, j: (l, 0, j)),
                  pl.BlockSpec((1, 1, tn), lambda l, j: (l, 0, j))],
        out_specs=pl.BlockSpec((1, MOD_ROWS, tn), lambda l, j: (l, 0, j)),
        out_shape=jax.ShapeDtypeStruct((depth, MOD_ROWS, n), F32),
        compiler_params=_params(2),
        name="ada",
    )(c_rows, w_ada, b_ada)


def _rope_partner(z):
    lane = lax.broadcasted_iota(jnp.int32, z.shape, 1)
    return jnp.where((lane & 32) == 0, pltpu.roll(z, 96, 1), pltpu.roll(z, 32, 1))


def _inproj_kernel(x_ref, shift_ref, scale_ref, wqk_ref, wg_ref, wv_ref, gain_ref, ones_ref, cos_ref, sin_ref,
                   oqk_ref, og_ref, ov_ref, *, use_rope):
    hb = _rms_modulate(x_ref[0], shift_ref[0], scale_ref[0]).astype(BF16)
    for c in range(G_COLS // 512):
        cols = slice(c * 512, (c + 1) * 512)
        og_ref[0, :, cols] = _sigmoid(jnp.dot(hb, wg_ref[:, cols], preferred_element_type=F32)).astype(BF16)
    ov_ref[0] = jnp.dot(hb, wv_ref[...], preferred_element_type=F32).astype(BF16)
    for c in range(QK_COLS // NORM_CHUNK):
        cols = slice(c * NORM_CHUNK, (c + 1) * NORM_CHUNK)
        y = jnp.dot(hb, wqk_ref[:, cols], preferred_element_type=F32)
        ss = jnp.dot((y * y).astype(BF16), ones_ref[...], preferred_element_type=F32)
        yn = y * lax.rsqrt(ss * (1.0 / HEAD_DIM) + EPS) * gain_ref[:, cols]
        if use_rope:
            halves = []
            for k in range(NORM_CHUNK // LANES):
                z = yn[:, k * LANES:(k + 1) * LANES]
                halves.append(z * cos_ref[...] + _rope_partner(z) * sin_ref[...])
            yn = jnp.concatenate(halves, axis=1)
        oqk_ref[0, :, cols] = yn.astype(BF16)


def _inproj(x, shift, scale, wqk, wg, wv, gain, ones, cos, sin, *, use_rope, tm):
    bx, t, d = x.shape
    row = lambda b, i: (b, 0, 0)
    tile = lambda b, i: (b, i, 0)
    return pl.pallas_call(
        functools.partial(_inproj_kernel, use_rope=use_rope),
        grid=(bx, t // tm),
        in_specs=[pl.BlockSpec((1, tm, d), tile),
                  pl.BlockSpec((1, 1, d), row), pl.BlockSpec((1, 1, d), row),
                  _resident((d, QK_COLS)), _resident((d, G_COLS)), _resident((d, V_COLS)),
                  _resident((1, QK_COLS)), _resident((NORM_CHUNK, NORM_CHUNK)),
                  pl.BlockSpec((tm, LANES), lambda b, i: (i, 0)), pl.BlockSpec((tm, LANES), lambda b, i: (i, 0))],
        out_specs=[pl.BlockSpec((1, tm, QK_COLS), tile), pl.BlockSpec((1, tm, G_COLS), tile),
                   pl.BlockSpec((1, tm, V_COLS), tile)],
        out_shape=[jax.ShapeDtypeStruct((bx, t, QK_COLS), BF16), jax.ShapeDtypeStruct((bx, t, G_COLS), BF16),
                   jax.ShapeDtypeStruct((bx, t, V_COLS), BF16)],
        compiler_params=_params(2),
        name="inproj_rope" if use_rope else "inproj_ctx",
    )(x, shift, scale, wqk, wg, wv, gain, ones, cos, sin)


def _qk(qs, k):
    return lax.dot_general(qs, k, (((1,), (1,)), ((), ())), preferred_element_type=F32)


def _online_softmax(streams, state, kl_ref, vl_ref, kc_ref, vc_ref, n_lat, tk):
    n = len(streams)
    state = [state[4 * i:4 * i + 4] for i in range(n)]
    for (qs, _, m0, l0), (qs_ref, m_ref, l_ref, acc_ref) in zip(streams, state):
        lane = lax.broadcasted_iota(jnp.int32, m_ref.shape, 1)
        qs_ref[...] = qs
        m_ref[...] = jnp.broadcast_to(m0, m_ref.shape)
        l_ref[...] = jnp.where(lane == 0, l0, 0.0)
        acc_ref[...] = jnp.zeros(acc_ref.shape, F32)

    def scores(i, key_block, width):
        qs = state[i][0][...]
        return [_qk(qs, key_block(c)) for c in range(width // SUB)]

    def update(i, subs, value_block):
        _, m_ref, l_ref, acc_ref = state[i]
        m = m_ref[...]
        top = subs[0]
        for s in subs[1:]:
            top = jnp.maximum(top, s)
        top = functools.reduce(jnp.maximum, [top[:, k * LANES:(k + 1) * LANES] for k in range(SUB // LANES)])
        m_new = jnp.maximum(m, jnp.max(top, axis=-1, keepdims=True))
        alpha = jnp.exp2(m - m_new)
        m_wide = jnp.concatenate([m_new] * (SUB // LANES), axis=1)
        l = alpha * l_ref[...]
        acc = alpha * acc_ref[...]
        for c, s in enumerate(subs):
            p = jnp.exp2(s - m_wide)
            for k in range(SUB // LANES):
                l = l + p[:, k * LANES:(k + 1) * LANES]
            acc = acc + jnp.dot(p.astype(BF16), value_block(c), preferred_element_type=F32)
        m_ref[...] = m_new
        l_ref[...] = l
        acc_ref[...] = acc

    def step(key_block, value_block, width):
        subs = [scores(i, functools.partial(key_block, lanes), width) for i, (_, lanes, _, _) in enumerate(streams)]
        for i, (_, lanes, _, _) in enumerate(streams):
            update(i, subs[i], functools.partial(value_block, lanes))

    if n_lat:
        def body(j, carry):
            rows = lambda c: pl.ds(pl.multiple_of(j * tk + c * SUB, SUB), SUB)
            step(lambda lanes, c: kl_ref[0, rows(c), lanes], lambda lanes, c: vl_ref[0, rows(c), lanes], tk)
            return carry
        lax.fori_loop(0, n_lat, body, 0)
    step(lambda lanes, c: kc_ref[0, c * SUB:(c + 1) * SUB, lanes],
         lambda lanes, c: vc_ref[0, c * SUB:(c + 1) * SUB, lanes], kc_ref.shape[1])
    return [acc_ref[...] / jnp.sum(l_ref[...], axis=-1, keepdims=True) for _, _, l_ref, acc_ref in state]


def _softmax_state(n_streams, rows):
    return [pltpu.VMEM((rows, LANES), BF16), pltpu.VMEM((rows, LANES), F32), pltpu.VMEM((rows, LANES), F32),
            pltpu.VMEM((rows, LANES), F32)] * n_streams


def _stack_pair(q_ref, c, tq, row0=0):
    g = c // 2
    lane = lax.broadcasted_iota(jnp.int32, (tq, LANES), 1)
    keep = (lane >= HEAD_DIM) if g else (lane < HEAD_DIM)
    z = q_ref[0, row0:row0 + tq, c * LANES:(c + 1) * LANES].astype(F32)
    swapped = pltpu.roll(z, HEAD_DIM, 1)
    first, second = (z, swapped) if g == 0 else (swapped, z)
    return jnp.concatenate([jnp.where(keep, first, 0.0).astype(BF16), jnp.where(keep, second, 0.0).astype(BF16)], axis=0)


def _unstack_pair(o, c, tq):
    lane = lax.broadcasted_iota(jnp.int32, (tq, LANES), 1)
    a, b = o[:tq], o[tq:]
    if c // 2 == 0:
        b = pltpu.roll(b, HEAD_DIM, 1)
    else:
        a = pltpu.roll(a, HEAD_DIM, 1)
    return jnp.where(lane < HEAD_DIM, a, b)


def _sink_rows(sink_ref, heads, tq):
    return jnp.concatenate([jnp.full((tq, 1), sink_ref[h], F32) for h in heads], axis=0)


def _gqa_kernel(sink_ref, q_ref, kl_ref, vl_ref, kc_ref, vc_ref, o_ref, *state, tq, tk, n_lat, use_sink):
    streams = []
    for c in range(4):
        if use_sink:
            m0, l0 = _sink_rows(sink_ref, (2 * c, 2 * c + 1), tq), jnp.ones((2 * tq, 1), F32)
        else:
            m0, l0 = jnp.full((2 * tq, 1), NEG, F32), jnp.zeros((2 * tq, 1), F32)
        streams.append((_stack_pair(q_ref, c, tq), slice(0, LANES), m0, l0))
    outs = _online_softmax(streams, state, kl_ref, vl_ref, kc_ref, vc_ref, n_lat, tk)
    for c, o in enumerate(outs):
        o_ref[0, :, c * LANES:(c + 1) * LANES] = _unstack_pair(o, c, tq).astype(BF16)


def _diff_kernel(lam_ref, q_ref, kl_ref, vl_ref, kc_ref, vc_ref, gain_ref, o_ref, *state, tq, tk, n_lat, out_scale):
    lane = lax.broadcasted_iota(jnp.int32, (tq, LANES), 1)
    streams = []
    for n in range(4):
        lanes = slice(n * LANES, (n + 1) * LANES)
        z = q_ref[0, :, lanes]
        zero = jnp.zeros_like(z)
        qs = jnp.concatenate([jnp.where(lane < HEAD_DIM, z, zero), jnp.where(lane >= HEAD_DIM, z, zero)], axis=0)
        streams.append((qs, lanes, jnp.full((2 * tq, 1), NEG, F32), jnp.zeros((2 * tq, 1), F32)))
    outs = _online_softmax(streams, state, kl_ref, vl_ref, kc_ref, vc_ref, n_lat, tk)
    for n, o in enumerate(outs):
        d = o[:tq] - lam_ref[0] * o[tq:]
        d = d * lax.rsqrt(jnp.mean(d * d, axis=-1, keepdims=True) + EPS) * gain_ref[...] * out_scale
        o_ref[0, :, n * LANES:(n + 1) * LANES] = d.astype(BF16)


def _lane_blocks(a):
    return [a[:, k * LANES:(k + 1) * LANES] for k in range(a.shape[1] // LANES)]


def _window_kernel(sink_ref, q_ref, kl_ref, vl_ref, kc_ref, vc_ref, o_ref, *, tq, nb, length):
    span = tq + 2 * WINDOW
    kc = kc_ref[0]
    vc = vc_ref[0]
    rel = lax.broadcasted_iota(jnp.int32, (tq, span), 1) - lax.broadcasted_iota(jnp.int32, (tq, span), 0)
    chains = [(j, c) for j in range(nb) for c in range(4)]
    starts = []
    for j in range(nb):
        first = (pl.program_id(1) * nb + j) * tq
        starts.append((first, pl.multiple_of(jnp.clip(first - WINDOW, 0, length - span), LANES)))
    sw, sc = [], []
    for j, c in chains:
        first, start = starts[j]
        valid = jnp.abs(rel + (start - first)) <= WINDOW
        qs = _stack_pair(q_ref, c, tq, j * tq)
        sw.append(jnp.where(valid[None], _qk(qs, kl_ref[0, pl.ds(start, span), :]).reshape(2, tq, span),
                            NEG).reshape(2 * tq, span))
        sc.append(_qk(qs, kc))
    snk = [_sink_rows(sink_ref, (2 * c, 2 * c + 1), tq) for _, c in chains]
    top = [jnp.max(functools.reduce(jnp.maximum, _lane_blocks(a) + _lane_blocks(b)), axis=-1, keepdims=True)
           for a, b in zip(sw, sc)]
    m = [jnp.maximum(t, s) for t, s in zip(top, snk)]
    pw = [jnp.exp2(a - mm) for a, mm in zip(sw, m)]
    pc = [jnp.exp2(b - mm) for b, mm in zip(sc, m)]
    tot = [jnp.sum(functools.reduce(jnp.add, _lane_blocks(a) + _lane_blocks(b)), axis=-1, keepdims=True)
           for a, b in zip(pw, pc)]
    l = [t + jnp.exp2(s - mm) for t, s, mm in zip(tot, snk, m)]
    o = [jnp.dot(a.astype(BF16), vl_ref[0, pl.ds(starts[j][1], span), :], preferred_element_type=F32)
         + jnp.dot(b.astype(BF16), vc, preferred_element_type=F32) for (j, _), a, b in zip(chains, pw, pc)]
    o = [a / ll for a, ll in zip(o, l)]
    for (j, c), a in zip(chains, o):
        o_ref[0, j * tq:(j + 1) * tq, c * LANES:(c + 1) * LANES] = _unstack_pair(a, c, tq).astype(BF16)


_SMEM = pl.BlockSpec(memory_space=pltpu.SMEM)


def _kv_specs(klat, kctx, k_blk, v_blk, width):
    s_lat, s_ctx = klat.shape[1], kctx.shape[1]
    return [pl.BlockSpec((1, s_lat, width), lambda b, i: (b, 0, k_blk)),
            pl.BlockSpec((1, s_lat, width), lambda b, i: (b, 0, v_blk)),
            pl.BlockSpec((1, s_ctx, width), lambda b, i: (b, 0, k_blk)),
            pl.BlockSpec((1, s_ctx, width), lambda b, i: (b, 0, v_blk))]


def _gqa(sink, q, klat, vlat, kctx, vctx, *, q_blk, k_blk, v_blk, tq, tk, n_lat, use_sink, name):
    bx, t, _ = q.shape
    return pl.pallas_call(
        functools.partial(_gqa_kernel, tq=tq, tk=tk, n_lat=n_lat, use_sink=use_sink),
        grid=(bx, t // tq),
        in_specs=[_SMEM, pl.BlockSpec((1, tq, 512), lambda b, i: (b, i, q_blk))]
        + _kv_specs(klat, kctx, k_blk, v_blk, LANES),
        out_specs=pl.BlockSpec((1, tq, 512), lambda b, i: (b, i, 0)),
        out_shape=jax.ShapeDtypeStruct((bx, t, 512), BF16),
        scratch_shapes=_softmax_state(4, 2 * tq),
        compiler_params=_params(2),
        name=name,
    )(sink, q, klat, vlat, kctx, vctx)


def _diff(lam, q, klat, vlat, kctx, vctx, gain, *, tq, tk, n_lat, out_scale, name):
    bx, t, _ = q.shape
    return pl.pallas_call(
        functools.partial(_diff_kernel, tq=tq, tk=tk, n_lat=n_lat, out_scale=out_scale),
        grid=(bx, t // tq),
        in_specs=[_SMEM, pl.BlockSpec((1, tq, 512), lambda b, i: (b, i, 1))]
        + _kv_specs(klat, kctx, 3, 0, 512) + [pl.BlockSpec((1, LANES), lambda b, i: (0, 0))],
        out_specs=pl.BlockSpec((1, tq, 512), lambda b, i: (b, i, 0)),
        out_shape=jax.ShapeDtypeStruct((bx, t, 512), BF16),
        scratch_shapes=_softmax_state(4, 2 * tq),
        compiler_params=_params(2),
        name=name,
    )(lam, q, klat, vlat, kctx, vctx, gain)


def _window(sink, q, klat, vlat, kctx, vctx, *, tq, nb):
    bx, t, _ = q.shape
    return pl.pallas_call(
        functools.partial(_window_kernel, tq=tq, nb=nb, length=t),
        grid=(bx, t // (nb * tq)),
        in_specs=[_SMEM, pl.BlockSpec((1, nb * tq, 512), lambda b, i: (b, i, 2))]
        + _kv_specs(klat, kctx, 17, 5, LANES),
        out_specs=pl.BlockSpec((1, nb * tq, 512), lambda b, i: (b, i, 0)),
        out_shape=jax.ShapeDtypeStruct((bx, t, 512), BF16),
        compiler_params=_params(2),
        name="attn_window",
    )(sink, q, klat, vlat, kctx, vctx)


def _route(logits, tri, base):
    lane = lax.broadcasted_iota(jnp.int32, logits.shape, 1).astype(F32)
    first = lambda hit: jnp.min(jnp.where(hit, lane, float(LANES)), axis=-1, keepdims=True)
    gl = jnp.where(lane < N_GROUPS, logits, NEG)
    gmax = jnp.max(gl, axis=-1, keepdims=True)
    gidx = first(gl == gmax)
    gw = 1.0 / jnp.sum(jnp.exp(gl - gmax), axis=-1, keepdims=True)
    lo = N_GROUPS + gidx * EXPERTS_PER_GROUP
    el = jnp.where((lane >= lo) & (lane < lo + EXPERTS_PER_GROUP), logits, NEG)
    v1 = jnp.max(el, axis=-1, keepdims=True)
    i1 = first(el == v1)
    el = jnp.where(lane == i1, NEG, el)
    v2 = jnp.max(el, axis=-1, keepdims=True)
    i2 = first(el == v2)
    e = jnp.exp(v2 - v1)
    w1 = gw / (1.0 + e)
    w2 = gw * e / (1.0 + e)
    e1 = i1 - N_GROUPS
    e2 = i2 - N_GROUPS
    hit1 = jnp.where(lane == e1, 1.0, 0.0)
    hit2 = jnp.where(lane == e2, 1.0, 0.0)
    before1 = jnp.dot(tri, hit1.astype(BF16), preferred_element_type=F32) + base
    base = base + jnp.sum(hit1, axis=0, keepdims=True)
    before2 = jnp.dot(tri, hit2.astype(BF16), preferred_element_type=F32) + base
    base = base + jnp.sum(hit2, axis=0, keepdims=True)
    r1 = jnp.sum(hit1 * before1, axis=-1, keepdims=True)
    r2 = jnp.sum(hit2 * before2, axis=-1, keepdims=True)
    out = jnp.zeros_like(logits)
    for k, val in enumerate((e1, e2, w1, w2, r1, r2)):
        out = jnp.where(lane == k, val, out)
    return out, base


def _merge_kernel(ya_ref, yb_ref, yc_ref, g_ref, x_ref, gate_ref, shift_ref, scale_ref,
                  wa_ref, wb_ref, wc_ref, wo_ref, wr_ref, tri_ref, base_ref, xo_ref, h_ref, r_ref, cnt_ref):
    d = D_MODEL

    @pl.when((pl.program_id(0) == 0) & (pl.program_id(1) == 0))
    def _():
        cnt_ref[...] = base_ref[...]

    m = None
    for k, (y_ref, w_ref) in enumerate(((ya_ref, wa_ref), (yb_ref, wb_ref), (yc_ref, wc_ref))):
        t = g_ref[0, :, k * d:(k + 1) * d].astype(F32) * jnp.dot(y_ref[0], w_ref[...], preferred_element_type=F32)
        m = t if m is None else m + t
    xn = x_ref[0] + gate_ref[0] * jnp.dot(m.astype(BF16), wo_ref[...], preferred_element_type=F32)
    xo_ref[0] = xn
    h2 = _rms_modulate(xn, shift_ref[0], scale_ref[0])
    h_ref[0] = h2
    hi = h2.astype(BF16)
    lo = (h2 - hi.astype(F32)).astype(BF16)
    both = jnp.dot(hi, wr_ref[...], preferred_element_type=F32)
    logits = (both[:, :LANES] + both[:, LANES:]) + jnp.dot(lo, wr_ref[:, :LANES], preferred_element_type=F32)
    r_ref[0], cnt_ref[...] = _route(logits, tri_ref[...], cnt_ref[...])


def _merge(ya, yb, yc, g, x, gate, shift, scale, wa, wb, wc, wo, wr, base, *, tm, name):
    bx, t, d = x.shape
    row = lambda b, i: (b, 0, 0)
    tile = lambda b, i: (b, i, 0)
    ids = jnp.arange(tm, dtype=jnp.int32)
    tri = (ids[None, :] < ids[:, None]).astype(BF16)
    return pl.pallas_call(
        _merge_kernel,
        grid=(bx, t // tm),
        in_specs=[pl.BlockSpec((1, tm, 512), tile)] * 3
        + [pl.BlockSpec((1, tm, G_COLS), tile), pl.BlockSpec((1, tm, d), tile)]
        + [pl.BlockSpec((1, 1, d), row)] * 3
        + [_resident((512, d))] * 3 + [_resident((d, d)), _resident((d, 2 * LANES)), _resident((tm, tm)),
                                      _resident((1, LANES))],
        out_specs=[pl.BlockSpec((1, tm, d), tile), pl.BlockSpec((1, tm, d), tile), pl.BlockSpec((1, tm, LANES), tile),
                   pl.BlockSpec((1, LANES), lambda b, i: (0, 0))],
        out_shape=[jax.ShapeDtypeStruct((bx, t, d), F32), jax.ShapeDtypeStruct((bx, t, d), F32),
                   jax.ShapeDtypeStruct((bx, t, LANES), F32), jax.ShapeDtypeStruct((1, LANES), F32)],
        compiler_params=_params(2),
        name=name,
    )(ya, yb, yc, g, x, gate, shift, scale, wa, wb, wc, wo, wr, tri, base)


def _dispatch_kernel(dest_ref, h_ref, xs_in_ref, xs_ref, sem, *, tokens):
    del xs_in_ref

    def row_copy(r, d):
        return pltpu.make_async_copy(h_ref.at[pl.ds(r, 1), :], xs_ref.at[pl.ds(d, 1), :], sem)

    def issue(r, carry):
        for k in range(2):
            row_copy(r, dest_ref[0, 0, 2 * r + k]).start()
        return carry

    lax.fori_loop(0, tokens, issue, 0)
    for _ in range(2):
        pltpu.make_async_copy(h_ref, xs_ref.at[pl.ds(0, tokens), :], sem).wait()


def _dispatch(dest, h, xs, *, td):
    n, d = h.shape
    return pl.pallas_call(
        functools.partial(_dispatch_kernel, tokens=td),
        grid=(n // td,),
        in_specs=[pl.BlockSpec((1, 1, 2 * td), lambda i: (i, 0, 0), memory_space=pltpu.SMEM),
                  pl.BlockSpec((td, d), lambda i: (i, 0)),
                  pl.BlockSpec(memory_space=pl.ANY)],
        out_specs=pl.BlockSpec(memory_space=pl.ANY),
        out_shape=jax.ShapeDtypeStruct(xs.shape, xs.dtype),
        scratch_shapes=[pltpu.SemaphoreType.DMA(())],
        input_output_aliases={2: 0},
        compiler_params=_params(1),
        name="moe_dispatch",
    )(dest.reshape(n // td, 1, 2 * td), h, xs)


def _experts_kernel(te_ref, used_ref, xs_ref, wg_ref, wu_ref, wd_ref, y_ref, wg_bf, wu_bf, wd_bf):
    t = pl.program_id(0)
    live = t < used_ref[0]

    @pl.when(live & ((t == 0) | (te_ref[t] != te_ref[jnp.maximum(t - 1, 0)])))
    def _():
        wg_bf[...] = wg_ref[0, 0].astype(BF16)
        wu_bf[...] = wu_ref[0, 0].astype(BF16)
        wd_bf[...] = wd_ref[0, 0].astype(BF16)

    @pl.when(live)
    def _():
        xb = xs_ref[...].astype(BF16)
        a = jnp.dot(xb, wg_bf[...], preferred_element_type=F32)
        u = jnp.dot(xb, wu_bf[...], preferred_element_type=F32)
        y_ref[...] = jnp.dot((a * _sigmoid(a) * u).astype(BF16), wd_bf[...], preferred_element_type=F32)

    @pl.when(t >= used_ref[0])
    def _():
        y_ref[...] = jnp.zeros_like(y_ref)


def _experts(tile_expert, n_used, xs, wg, wu, wd, *, layer, tm):
    p, d = xs.shape
    hid = wg.shape[3]
    weights = lambda t, te, nu: (layer, te[t], 0, 0)
    return pl.pallas_call(
        _experts_kernel,
        grid_spec=pltpu.PrefetchScalarGridSpec(
            num_scalar_prefetch=2,
            grid=(p // tm,),
            in_specs=[pl.BlockSpec((tm, d), lambda t, te, nu: (t, 0)),
                      pl.BlockSpec((1, 1, d, hid), weights), pl.BlockSpec((1, 1, d, hid), weights),
                      pl.BlockSpec((1, 1, hid, d), weights)],
            out_specs=pl.BlockSpec((tm, d), lambda t, te, nu: (t, 0)),
            scratch_shapes=[pltpu.VMEM((d, hid), BF16), pltpu.VMEM((d, hid), BF16), pltpu.VMEM((hid, d), BF16)]),
        out_shape=jax.ShapeDtypeStruct((p, d), F32),
        compiler_params=_params(1),
        name="moe_experts",
    )(tile_expert, n_used, xs, wg, wu, wd)


def _combine_kernel(dest_ref, y_ref, x_ref, gate_ref, r_ref, o_ref, buf, sem, *, tokens):
    def row_copy(r, k, d):
        return pltpu.make_async_copy(y_ref.at[pl.ds(d, 1), :], buf.at[k, pl.ds(r, 1), :], sem)

    def issue(r, carry):
        for k in range(2):
            row_copy(r, k, dest_ref[0, 0, 2 * r + k]).start()
        return carry

    lax.fori_loop(0, tokens, issue, 0)
    for k in range(2):
        pltpu.make_async_copy(y_ref.at[pl.ds(0, tokens), :], buf.at[k], sem).wait()
    w = r_ref[...]
    o_ref[0] = x_ref[0] + gate_ref[0] * (w[:, 2:3] * buf[0] + w[:, 3:4] * buf[1])


def _combine(dest, y, x, gate, route, *, tc, name):
    bx, t, d = x.shape
    per = t // tc
    return pl.pallas_call(
        functools.partial(_combine_kernel, tokens=tc),
        grid=(bx * per,),
        in_specs=[pl.BlockSpec((1, 1, 2 * tc), lambda i: (i, 0, 0), memory_space=pltpu.SMEM),
                  pl.BlockSpec(memory_space=pl.ANY),
                  pl.BlockSpec((1, tc, d), lambda i: (i // per, i % per, 0)),
                  pl.BlockSpec((1, 1, d), lambda i: (i // per, 0, 0)),
                  pl.BlockSpec((tc, LANES), lambda i: (i, 0))],
        out_specs=pl.BlockSpec((1, tc, d), lambda i: (i // per, i % per, 0)),
        out_shape=jax.ShapeDtypeStruct(x.shape, F32),
        scratch_shapes=[pltpu.VMEM((2, tc, d), F32), pltpu.SemaphoreType.DMA(())],
        compiler_params=_params(1),
        name=name,
    )(dest.reshape(bx * per, 1, 2 * tc), y, x, gate, route)


def _expert_slots(routes, counts, tm):
    n_assign = 2 * sum(r.shape[0] for r in routes)
    padded = ((counts + tm - 1) // tm) * tm
    ends = jnp.cumsum(padded)
    starts = ends - padded
    experts = jnp.arange(N_EXPERTS, dtype=jnp.int32)
    dests = []
    for r in routes:
        e = r[:, 0:2].astype(jnp.int32)
        rank = r[:, 4:6].astype(jnp.int32)
        start = jnp.sum(jnp.where(e[:, :, None] == experts, starts, 0), axis=-1)
        dests.append((start + rank).reshape(-1))
    n_tiles = n_assign // tm + N_EXPERTS
    tile_expert = jnp.minimum(jnp.searchsorted(ends // tm, jnp.arange(n_tiles, dtype=jnp.int32), side="right"),
                              N_EXPERTS - 1).astype(jnp.int32)
    return dests, tile_expert, (ends[-1:] // tm).astype(jnp.int32), n_tiles


def _rope_tables(length):
    pairs = HEAD_DIM // 4
    pos = jnp.arange(length, dtype=jnp.int32)
    row = (pos // GRID_W).astype(F32)
    col = (pos % GRID_W).astype(F32)
    freqs = ROPE_THETA ** (-jnp.arange(pairs, dtype=F32) / pairs)
    ang = jnp.concatenate([row[:, None] * freqs, col[:, None] * freqs], axis=-1)
    cos = jnp.tile(jnp.cos(ang), (1, 4))
    sin = jnp.tile(jnp.concatenate([-jnp.sin(ang), jnp.sin(ang)], axis=-1), (1, 2))
    return cos, sin


def kernel(x, c, ctx, c_ctx, w_ada, b_ada, w_in, a_qnorm, a_knorm, b_qnorm, b_knorm, c_qnorm, c_knorm, lambda_q1, lambda_k1, lambda_q2, lambda_k2, b_subln, c_sink, w_branch_a, w_branch_b, w_branch_c, w_out, w_router_group, w_router_expert, w_exp_gate, w_exp_up, w_exp_down):
    bsz, length, d = x.shape
    c_len = ctx.shape[1]
    depth = w_ada.shape[0]
    assert d == D_MODEL and bsz + 1 <= MOD_ROWS and length % 512 == 0 and c_len == 256

    cos, sin = _rope_tables(length)
    no_rope = jnp.zeros((c_len, LANES), F32)
    head_ids = jnp.arange(NORM_CHUNK, dtype=jnp.int32) // HEAD_DIM
    ones = (head_ids[:, None] == head_ids[None, :]).astype(BF16)
    c_rows = jnp.concatenate([c, c_ctx[None, :], jnp.zeros((MOD_ROWS - bsz - 1, d), F32)], axis=0)
    mod_all = _ada(c_rows, w_ada, b_ada.reshape(depth, 1, -1))
    zero_sink = jnp.zeros((8,), F32)

    tq_a, tq_b = 256, 256
    tk = 1024 if length % 1024 == 0 else 512
    n_lat = length // tk
    tm_moe = 256

    for l in range(depth):
        need_ctx = l < depth - 1
        mod = lambda k: mod_all[l, :bsz, k * d:(k + 1) * d].reshape(bsz, 1, d)
        cmod = lambda k: jnp.broadcast_to(mod_all[l, bsz, k * d:(k + 1) * d], (bsz, 1, d))
        lam_init = 0.8 - 0.6 * math.exp(-0.3 * l)
        lam = (jnp.exp(jnp.sum(lambda_q1[l] * lambda_k1[l])) - jnp.exp(jnp.sum(lambda_q2[l] * lambda_k2[l]))
               + lam_init).reshape(1).astype(F32)

        w = w_in[l]
        wqk = jnp.concatenate([w[:, 0:1536], w[:, 4864:5376], w[:, 4608:4736], w[:, 5888:6016]], axis=1).astype(BF16)
        wg = w[:, 1536:4608].astype(BF16)
        wv = jnp.concatenate([w[:, 5376:5888], w[:, 4736:4864], w[:, 6016:6144]], axis=1).astype(BF16)
        q_scale = HEAD_DIM ** -0.5 * LOG2E
        gain = jnp.concatenate([jnp.tile(a_qnorm[l], 8) * q_scale, jnp.tile(b_qnorm[l], 8) * q_scale,
                                jnp.tile(c_qnorm[l], 8) * q_scale, jnp.tile(b_knorm[l], 8),
                                jnp.tile(a_knorm[l], 2), jnp.tile(c_knorm[l], 2)]).reshape(1, QK_COLS)
        subln = b_subln[l].reshape(1, LANES)
        sink = c_sink[l] * LOG2E
        w_router = jnp.concatenate([w_router_group[l], w_router_expert[l],
                                    jnp.zeros((d, LANES - N_GROUPS - N_EXPERTS), F32)], axis=1)
        w_router_hi = w_router.astype(BF16)
        w_router_lo = (w_router - w_router_hi.astype(F32)).astype(BF16)
        merge_w = (w_branch_a[l].astype(BF16), w_branch_b[l].astype(BF16), w_branch_c[l].astype(BF16),
                   w_out[l].astype(BF16), jnp.concatenate([w_router_hi, w_router_lo], axis=1))

        qk, gates, v = _inproj(x, mod(0), mod(1), wqk, wg, wv, gain, ones, cos, sin, use_rope=True, tm=512)
        cqk, cgates, cv = _inproj(ctx, cmod(0), cmod(1), wqk, wg, wv, gain, ones, no_rope, no_rope,
                                  use_rope=False, tm=c_len)
        ya = _gqa(zero_sink, qk, qk, v, cqk, cv, q_blk=0, k_blk=16, v_blk=4, tq=tq_a, tk=tk, n_lat=n_lat,
                  use_sink=False, name="attn_gqa")
        yb = _diff(lam, qk, qk, v, cqk, cv, subln, tq=tq_b, tk=tk, n_lat=n_lat, out_scale=1.0 - lam_init,
                   name="attn_diff")
        yc = _window(sink, qk, qk, v, cqk, cv, tq=128, nb=4)
        x, h2, route, counts = _merge(ya, yb, yc, gates, x, mod(2), mod(3), mod(4), *merge_w,
                                      jnp.zeros((1, LANES), F32), tm=512, name="merge")
        tokens = [h2.reshape(bsz * length, d)]
        routes = [route.reshape(bsz * length, LANES)]
        if need_ctx:
            cya = _gqa(zero_sink, cqk, cqk, cv, cqk, cv, q_blk=0, k_blk=16, v_blk=4, tq=128, tk=tk, n_lat=0,
                       use_sink=False, name="ctx_gqa")
            cyb = _diff(lam, cqk, cqk, cv, cqk, cv, subln, tq=c_len, tk=tk, n_lat=0, out_scale=1.0 - lam_init,
                        name="ctx_diff")
            cyc = _gqa(sink, cqk, cqk, cv, cqk, cv, q_blk=2, k_blk=17, v_blk=5, tq=128, tk=tk, n_lat=0,
                       use_sink=True, name="ctx_sink")
            ctx, hc2, croute, counts = _merge(cya, cyb, cyc, cgates, ctx, cmod(2), cmod(3), cmod(4), *merge_w, counts,
                                              tm=c_len, name="ctx_merge")
            tokens.append(hc2.reshape(bsz * c_len, d))
            routes.append(croute.reshape(bsz * c_len, LANES))

        dests, tile_expert, n_used, n_tiles = _expert_slots(routes, counts[0, :N_EXPERTS].astype(jnp.int32), tm_moe)
        xs = jnp.zeros((n_tiles * tm_moe, d), F32)
        for t, dest in zip(tokens, dests):
            xs = _dispatch(dest, t, xs, td=256)
        y = _experts(tile_expert, n_used, xs, w_exp_gate, w_exp_up, w_exp_down, layer=l, tm=tm_moe)
        x = _combine(dests[0], y, x, mod(5), routes[0], tc=256, name="moe_combine")
        if need_ctx:
            ctx = _combine(dests[1], y, ctx, cmod(5), routes[1], tc=256, name="ctx_combine")
    return x
```

```python
import functools
import math

import jax
import jax.numpy as jnp
from jax import lax
from jax.experimental import pallas as pl
from jax.experimental.pallas import tpu as pltpu

F32 = jnp.float32
BF16 = jnp.bfloat16

D_MODEL = 1024
HEAD_DIM = 64
GRID_W = 64
ROPE_THETA = 10000.0
WINDOW = 128
N_GROUPS = 4
EXPERTS_PER_GROUP = 8
N_EXPERTS = N_GROUPS * EXPERTS_PER_GROUP
EXPERT_HIDDEN = D_MODEL // 2
N_MOD = 6
EPS = 1e-6
NEG = -1e30
LOG2E = math.log2(math.e)
LANES = 128
MOD_ROWS = 16

QK_COLS = 2304
G_COLS = 3 * D_MODEL
V_COLS = 768
NORM_CHUNK = 256
SUB = 256

VMEM_LIMIT = 48 * 1024 * 1024


def _params(n_axes, vmem=VMEM_LIMIT, row_dma=False):
    return pltpu.CompilerParams(dimension_semantics=("arbitrary",) * n_axes, vmem_limit_bytes=vmem,
                                disable_bounds_checks=row_dma)


def _resident(shape):
    return pl.BlockSpec(shape, lambda *_: (0,) * len(shape), pipeline_mode=pl.Buffered(1))


def _sigmoid(v):
    return 1.0 / (1.0 + jnp.exp(-v))


def _rms_modulate(v, shift, scale):
    v = v * lax.rsqrt(jnp.mean(v * v, axis=-1, keepdims=True) + EPS)
    return v * (1.0 + scale) + shift


def _ada_kernel(c_ref, w_ref, b_ref, o_ref):
    c = c_ref[...]
    o_ref[0] = jnp.dot(c * _sigmoid(c), w_ref[0], preferred_element_type=F32,
                       precision=lax.Precision.HIGHEST) + b_ref[0]


def _ada(c_rows, w_ada, b_ada):
    depth, d, n = w_ada.shape
    tn = 1536
    return pl.pallas_call(
        _ada_kernel,
        grid=(depth, n // tn),
        in_specs=[pl.BlockSpec((MOD_ROWS, d), lambda l, j: (0, 0)),
                  pl.BlockSpec((1, d, tn), lambda l, j: (l, 0, j)),
                  pl.BlockSpec((1, 1, tn), lambda l, j: (l, 0, j))],
        out_specs=pl.BlockSpec((1, MOD_ROWS, tn), lambda l, j: (l, 0, j)),
        out_shape=jax.ShapeDtypeStruct((depth, MOD_ROWS, n), F32),
        compiler_params=_params(2),
        name="ada",
    )(c_rows, w_ada, b_ada)


def _rope_partner(z):
    lane = lax.broadcasted_iota(jnp.int32, z.shape, 1)
    return jnp.where((lane & 32) == 0, pltpu.roll(z, 96, 1), pltpu.roll(z, 32, 1))


def _inproj_kernel(x_ref, shift_ref, scale_ref, wqk_ref, wg_ref, wv_ref, gain_ref, ones_ref, cos_ref, sin_ref,
                   oqk_ref, og_ref, ov_ref, *, use_rope):
    hb = _rms_modulate(x_ref[0], shift_ref[0], scale_ref[0]).astype(BF16)
    chunks = [slice(c * NORM_CHUNK, (c + 1) * NORM_CHUNK) for c in range(QK_COLS // NORM_CHUNK)]
    ys = [jnp.dot(hb, wqk_ref[:, cols], preferred_element_type=F32) for cols in chunks]
    sss = [jnp.dot((y * y).astype(BF16), ones_ref[...], preferred_element_type=F32) for y in ys]
    for c in range(G_COLS // 512):
        cols = slice(c * 512, (c + 1) * 512)
        og_ref[0, :, cols] = _sigmoid(jnp.dot(hb, wg_ref[:, cols], preferred_element_type=F32)).astype(BF16)
    ov_ref[0] = jnp.dot(hb, wv_ref[...], preferred_element_type=F32).astype(BF16)
    for cols, y, ss in zip(chunks, ys, sss):
        yn = y * lax.rsqrt(ss * (1.0 / HEAD_DIM) + EPS) * gain_ref[:, cols]
        if use_rope:
            halves = []
            for k in range(NORM_CHUNK // LANES):
                z = yn[:, k * LANES:(k + 1) * LANES]
                halves.append(z * cos_ref[...] + _rope_partner(z) * sin_ref[...])
            yn = jnp.concatenate(halves, axis=1)
        oqk_ref[0, :, cols] = yn.astype(BF16)


def _inproj(x, shift, scale, wqk, wg, wv, gain, ones, cos, sin, *, use_rope, tm):
    bx, t, d = x.shape
    row = lambda b, i: (b, 0, 0)
    tile = lambda b, i: (b, i, 0)
    return pl.pallas_call(
        functools.partial(_inproj_kernel, use_rope=use_rope),
        grid=(bx, t // tm),
        in_specs=[pl.BlockSpec((1, tm, d), tile),
                  pl.BlockSpec((1, 1, d), row), pl.BlockSpec((1, 1, d), row),
                  _resident((d, QK_COLS)), _resident((d, G_COLS)), _resident((d, V_COLS)),
                  _resident((1, QK_COLS)), _resident((NORM_CHUNK, NORM_CHUNK)),
                  pl.BlockSpec((tm, LANES), lambda b, i: (i, 0)), pl.BlockSpec((tm, LANES), lambda b, i: (i, 0))],
        out_specs=[pl.BlockSpec((1, tm, QK_COLS), tile), pl.BlockSpec((1, tm, G_COLS), tile),
                   pl.BlockSpec((1, tm, V_COLS), tile)],
        out_shape=[jax.ShapeDtypeStruct((bx, t, QK_COLS), BF16), jax.ShapeDtypeStruct((bx, t, G_COLS), BF16),
                   jax.ShapeDtypeStruct((bx, t, V_COLS), BF16)],
        compiler_params=_params(2),
        name="inproj_rope" if use_rope else "inproj_ctx",
    )(x, shift, scale, wqk, wg, wv, gain, ones, cos, sin)


def _qk(qs, k):
    return lax.dot_general(qs, k, (((1,), (1,)), ((), ())), preferred_element_type=F32)


def _online_softmax(streams, state, kl_ref, vl_ref, kc_ref, vc_ref, n_lat, tk):
    n = len(streams)
    state = [state[4 * i:4 * i + 4] for i in range(n)]
    for (qs, _, m0, l0), (qs_ref, m_ref, l_ref, acc_ref) in zip(streams, state):
        lane = lax.broadcasted_iota(jnp.int32, m_ref.shape, 1)
        qs_ref[...] = qs
        m_ref[...] = jnp.broadcast_to(m0, m_ref.shape)
        l_ref[...] = jnp.where(lane == 0, l0, 0.0)
        acc_ref[...] = jnp.zeros(acc_ref.shape, F32)

    def scores(i, key_block, width):
        qs = state[i][0][...]
        return [_qk(qs, key_block(c)) for c in range(width // SUB)]

    def update(i, subs, value_block):
        _, m_ref, l_ref, acc_ref = state[i]
        m = m_ref[...]
        top = subs[0]
        for s in subs[1:]:
            top = jnp.maximum(top, s)
        top = functools.reduce(jnp.maximum, [top[:, k * LANES:(k + 1) * LANES] for k in range(SUB // LANES)])
        m_new = jnp.maximum(m, jnp.max(top, axis=-1, keepdims=True))
        alpha = jnp.exp2(m - m_new)
        m_wide = jnp.concatenate([m_new] * (SUB // LANES), axis=1)
        l = alpha * l_ref[...]
        acc = alpha * acc_ref[...]
        for c, s in enumerate(subs):
            p = jnp.exp2(s - m_wide)
            for k in range(SUB // LANES):
                l = l + p[:, k * LANES:(k + 1) * LANES]
            acc = acc + jnp.dot(p.astype(BF16), value_block(c), preferred_element_type=F32)
        m_ref[...] = m_new
        l_ref[...] = l
        acc_ref[...] = acc

    def step(key_block, value_block, width):
        subs = [scores(i, functools.partial(key_block, lanes), width) for i, (_, lanes, _, _) in enumerate(streams)]
        for i, (_, lanes, _, _) in enumerate(streams):
            update(i, subs[i], functools.partial(value_block, lanes))

    if n_lat:
        def body(j, carry):
            rows = lambda c: pl.ds(pl.multiple_of(j * tk + c * SUB, SUB), SUB)
            step(lambda lanes, c: kl_ref[0, rows(c), lanes], lambda lanes, c: vl_ref[0, rows(c), lanes], tk)
            return carry
        lax.fori_loop(0, n_lat, body, 0)
    step(lambda lanes, c: kc_ref[0, c * SUB:(c + 1) * SUB, lanes],
         lambda lanes, c: vc_ref[0, c * SUB:(c + 1) * SUB, lanes], kc_ref.shape[1])
    return [acc_ref[...] / jnp.sum(l_ref[...], axis=-1, keepdims=True) for _, _, l_ref, acc_ref in state]


def _softmax_state(n_streams, rows):
    return [pltpu.VMEM((rows, LANES), BF16), pltpu.VMEM((rows, LANES), F32), pltpu.VMEM((rows, LANES), F32),
            pltpu.VMEM((rows, LANES), F32)] * n_streams


def _stack_pair(q_ref, c, tq, row0=0):
    g = c // 2
    lane = lax.broadcasted_iota(jnp.int32, (tq, LANES), 1)
    keep = (lane >= HEAD_DIM) if g else (lane < HEAD_DIM)
    z = q_ref[0, row0:row0 + tq, c * LANES:(c + 1) * LANES].astype(F32)
    swapped = pltpu.roll(z, HEAD_DIM, 1)
    first, second = (z, swapped) if g == 0 else (swapped, z)
    return jnp.concatenate([jnp.where(keep, first, 0.0).astype(BF16), jnp.where(keep, second, 0.0).astype(BF16)], axis=0)


def _unstack_pair(o, c, tq):
    lane = lax.broadcasted_iota(jnp.int32, (tq, LANES), 1)
    a, b = o[:tq], o[tq:]
    if c // 2 == 0:
        b = pltpu.roll(b, HEAD_DIM, 1)
    else:
        a = pltpu.roll(a, HEAD_DIM, 1)
    return jnp.where(lane < HEAD_DIM, a, b)


def _sink_rows(sink_ref, heads, tq):
    return jnp.concatenate([jnp.full((tq, 1), sink_ref[h], F32) for h in heads], axis=0)


def _gqa_kernel(sink_ref, q_ref, kl_ref, vl_ref, kc_ref, vc_ref, o_ref, *state, tq, tk, n_lat, use_sink):
    streams = []
    for c in range(4):
        if use_sink:
            m0, l0 = _sink_rows(sink_ref, (2 * c, 2 * c + 1), tq), jnp.ones((2 * tq, 1), F32)
        else:
            m0, l0 = jnp.full((2 * tq, 1), NEG, F32), jnp.zeros((2 * tq, 1), F32)
        streams.append((_stack_pair(q_ref, c, tq), slice(0, LANES), m0, l0))
    outs = _online_softmax(streams, state, kl_ref, vl_ref, kc_ref, vc_ref, n_lat, tk)
    for c, o in enumerate(outs):
        o_ref[0, :, c * LANES:(c + 1) * LANES] = _unstack_pair(o, c, tq).astype(BF16)


def _diff_kernel(lam_ref, q_ref, kl_ref, vl_ref, kc_ref, vc_ref, gain_ref, o_ref, *state, tq, tk, n_lat, out_scale):
    lane = lax.broadcasted_iota(jnp.int32, (tq, LANES), 1)
    streams = []
    for n in range(4):
        lanes = slice(n * LANES, (n + 1) * LANES)
        z = q_ref[0, :, lanes]
        zero = jnp.zeros_like(z)
        qs = jnp.concatenate([jnp.where(lane < HEAD_DIM, z, zero), jnp.where(lane >= HEAD_DIM, z, zero)], axis=0)
        streams.append((qs, lanes, jnp.full((2 * tq, 1), NEG, F32), jnp.zeros((2 * tq, 1), F32)))
    outs = _online_softmax(streams, state, kl_ref, vl_ref, kc_ref, vc_ref, n_lat, tk)
    for n, o in enumerate(outs):
        d = o[:tq] - lam_ref[0] * o[tq:]
        d = d * lax.rsqrt(jnp.mean(d * d, axis=-1, keepdims=True) + EPS) * gain_ref[...] * out_scale
        o_ref[0, :, n * LANES:(n + 1) * LANES] = d.astype(BF16)


def _lane_blocks(a):
    return [a[:, k * LANES:(k + 1) * LANES] for k in range(a.shape[1] // LANES)]


def _window_kernel(sink_ref, q_ref, kl_ref, vl_ref, kc_ref, vc_ref, o_ref, *, tq, nb, length):
    span = tq + 2 * WINDOW
    kc = kc_ref[0]
    vc = vc_ref[0]
    rel = lax.broadcasted_iota(jnp.int32, (tq, span), 1) - lax.broadcasted_iota(jnp.int32, (tq, span), 0)
    chains = [(j, c) for j in range(nb) for c in range(4)]
    starts = []
    for j in range(nb):
        first = (pl.program_id(1) * nb + j) * tq
        starts.append((first, pl.multiple_of(jnp.clip(first - WINDOW, 0, length - span), LANES)))
    sw, sc = [], []
    for j, c in chains:
        first, start = starts[j]
        valid = jnp.abs(rel + (start - first)) <= WINDOW
        qs = _stack_pair(q_ref, c, tq, j * tq)
        sw.append(jnp.where(valid[None], _qk(qs, kl_ref[0, pl.ds(start, span), :]).reshape(2, tq, span),
                            NEG).reshape(2 * tq, span))
        sc.append(_qk(qs, kc))
    snk = [_sink_rows(sink_ref, (2 * c, 2 * c + 1), tq) for _, c in chains]
    top = [jnp.max(functools.reduce(jnp.maximum, _lane_blocks(a) + _lane_blocks(b)), axis=-1, keepdims=True)
           for a, b in zip(sw, sc)]
    m = [jnp.maximum(t, s) for t, s in zip(top, snk)]
    pw = [jnp.exp2(a - mm) for a, mm in zip(sw, m)]
    pc = [jnp.exp2(b - mm) for b, mm in zip(sc, m)]
    tot = [jnp.sum(functools.reduce(jnp.add, _lane_blocks(a) + _lane_blocks(b)), axis=-1, keepdims=True)
           for a, b in zip(pw, pc)]
    l = [t + jnp.exp2(s - mm) for t, s, mm in zip(tot, snk, m)]
    o = [jnp.dot(a.astype(BF16), vl_ref[0, pl.ds(starts[j][1], span), :], preferred_element_type=F32)
         + jnp.dot(b.astype(BF16), vc, preferred_element_type=F32) for (j, _), a, b in zip(chains, pw, pc)]
    o = [a / ll for a, ll in zip(o, l)]
    for (j, c), a in zip(chains, o):
        o_ref[0, j * tq:(j + 1) * tq, c * LANES:(c + 1) * LANES] = _unstack_pair(a, c, tq).astype(BF16)


_SMEM = pl.BlockSpec(memory_space=pltpu.SMEM)


def _kv_specs(klat, kctx, k_blk, v_blk, width):
    s_lat, s_ctx = klat.shape[1], kctx.shape[1]
    return [pl.BlockSpec((1, s_lat, width), lambda b, i: (b, 0, k_blk)),
            pl.BlockSpec((1, s_lat, width), lambda b, i: (b, 0, v_blk)),
            pl.BlockSpec((1, s_ctx, width), lambda b, i: (b, 0, k_blk)),
            pl.BlockSpec((1, s_ctx, width), lambda b, i: (b, 0, v_blk))]


def _gqa(sink, q, klat, vlat, kctx, vctx, *, q_blk, k_blk, v_blk, tq, tk, n_lat, use_sink, name):
    bx, t, _ = q.shape
    return pl.pallas_call(
        functools.partial(_gqa_kernel, tq=tq, tk=tk, n_lat=n_lat, use_sink=use_sink),
        grid=(bx, t // tq),
        in_specs=[_SMEM, pl.BlockSpec((1, tq, 512), lambda b, i: (b, i, q_blk))]
        + _kv_specs(klat, kctx, k_blk, v_blk, LANES),
        out_specs=pl.BlockSpec((1, tq, 512), lambda b, i: (b, i, 0)),
        out_shape=jax.ShapeDtypeStruct((bx, t, 512), BF16),
        scratch_shapes=_softmax_state(4, 2 * tq),
        compiler_params=_params(2),
        name=name,
    )(sink, q, klat, vlat, kctx, vctx)


def _diff(lam, q, klat, vlat, kctx, vctx, gain, *, tq, tk, n_lat, out_scale, name):
    bx, t, _ = q.shape
    return pl.pallas_call(
        functools.partial(_diff_kernel, tq=tq, tk=tk, n_lat=n_lat, out_scale=out_scale),
        grid=(bx, t // tq),
        in_specs=[_SMEM, pl.BlockSpec((1, tq, 512), lambda b, i: (b, i, 1))]
        + _kv_specs(klat, kctx, 3, 0, 512) + [pl.BlockSpec((1, LANES), lambda b, i: (0, 0))],
        out_specs=pl.BlockSpec((1, tq, 512), lambda b, i: (b, i, 0)),
        out_shape=jax.ShapeDtypeStruct((bx, t, 512), BF16),
        scratch_shapes=_softmax_state(4, 2 * tq),
        compiler_params=_params(2),
        name=name,
    )(lam, q, klat, vlat, kctx, vctx, gain)


def _window(sink, q, klat, vlat, kctx, vctx, *, tq, nb):
    bx, t, _ = q.shape
    return pl.pallas_call(
        functools.partial(_window_kernel, tq=tq, nb=nb, length=t),
        grid=(bx, t // (nb * tq)),
        in_specs=[_SMEM, pl.BlockSpec((1, nb * tq, 512), lambda b, i: (b, i, 2))]
        + _kv_specs(klat, kctx, 17, 5, LANES),
        out_specs=pl.BlockSpec((1, nb * tq, 512), lambda b, i: (b, i, 0)),
        out_shape=jax.ShapeDtypeStruct((bx, t, 512), BF16),
        compiler_params=_params(2),
        name="attn_window",
    )(sink, q, klat, vlat, kctx, vctx)


def _route(logits, tri, base):
    lane = lax.broadcasted_iota(jnp.int32, logits.shape, 1).astype(F32)
    first = lambda hit: jnp.min(jnp.where(hit, lane, float(LANES)), axis=-1, keepdims=True)
    gl = jnp.where(lane < N_GROUPS, logits, NEG)
    gmax = jnp.max(gl, axis=-1, keepdims=True)
    gidx = first(gl == gmax)
    gw = 1.0 / jnp.sum(jnp.exp(gl - gmax), axis=-1, keepdims=True)
    lo = N_GROUPS + gidx * EXPERTS_PER_GROUP
    el = jnp.where((lane >= lo) & (lane < lo + EXPERTS_PER_GROUP), logits, NEG)
    v1 = jnp.max(el, axis=-1, keepdims=True)
    i1 = first(el == v1)
    el = jnp.where(lane == i1, NEG, el)
    v2 = jnp.max(el, axis=-1, keepdims=True)
    i2 = first(el == v2)
    e = jnp.exp(v2 - v1)
    w1 = gw / (1.0 + e)
    w2 = gw * e / (1.0 + e)
    e1 = i1 - N_GROUPS
    e2 = i2 - N_GROUPS
    hit1 = jnp.where(lane == e1, 1.0, 0.0)
    hit2 = jnp.where(lane == e2, 1.0, 0.0)
    before1 = jnp.dot(tri, hit1.astype(BF16), preferred_element_type=F32) + base
    base = base + jnp.sum(hit1, axis=0, keepdims=True)
    before2 = jnp.dot(tri, hit2.astype(BF16), preferred_element_type=F32) + base
    base = base + jnp.sum(hit2, axis=0, keepdims=True)
    r1 = jnp.sum(hit1 * before1, axis=-1, keepdims=True)
    r2 = jnp.sum(hit2 * before2, axis=-1, keepdims=True)
    out = jnp.zeros_like(logits)
    for k, val in enumerate((e1, e2, w1, w2, r1, r2)):
        out = jnp.where(lane == k, val, out)
    return out, base


def _merge_kernel(ya_ref, yb_ref, yc_ref, g_ref, x_ref, gate_ref, shift_ref, scale_ref,
                  wa_ref, wb_ref, wc_ref, wo_ref, wr_ref, tri_ref, base_ref, xo_ref, h_ref, r_ref, cnt_ref):
    d = D_MODEL

    @pl.when((pl.program_id(0) == 0) & (pl.program_id(1) == 0))
    def _():
        cnt_ref[...] = base_ref[...]

    m = None
    for k, (y_ref, w_ref) in enumerate(((ya_ref, wa_ref), (yb_ref, wb_ref), (yc_ref, wc_ref))):
        t = g_ref[0, :, k * d:(k + 1) * d].astype(F32) * jnp.dot(y_ref[0], w_ref[...], preferred_element_type=F32)
        m = t if m is None else m + t
    xn = x_ref[0] + gate_ref[0] * jnp.dot(m.astype(BF16), wo_ref[...], preferred_element_type=F32)
    xo_ref[0] = xn
    h2 = _rms_modulate(xn, shift_ref[0], scale_ref[0])
    h_ref[0] = h2
    hi = h2.astype(BF16)
    lo = (h2 - hi.astype(F32)).astype(BF16)
    both = jnp.dot(hi, wr_ref[...], preferred_element_type=F32)
    logits = (both[:, :LANES] + both[:, LANES:]) + jnp.dot(lo, wr_ref[:, :LANES], preferred_element_type=F32)
    r_ref[0], cnt_ref[...] = _route(logits, tri_ref[...], cnt_ref[...])


def _merge(ya, yb, yc, g, x, gate, shift, scale, wa, wb, wc, wo, wr, base, *, tm, name):
    bx, t, d = x.shape
    row = lambda b, i: (b, 0, 0)
    tile = lambda b, i: (b, i, 0)
    ids = jnp.arange(tm, dtype=jnp.int32)
    tri = (ids[None, :] < ids[:, None]).astype(BF16)
    return pl.pallas_call(
        _merge_kernel,
        grid=(bx, t // tm),
        in_specs=[pl.BlockSpec((1, tm, 512), tile)] * 3
        + [pl.BlockSpec((1, tm, G_COLS), tile), pl.BlockSpec((1, tm, d), tile)]
        + [pl.BlockSpec((1, 1, d), row)] * 3
        + [_resident((512, d))] * 3 + [_resident((d, d)), _resident((d, 2 * LANES)), _resident((tm, tm)),
                                      _resident((1, LANES))],
        out_specs=[pl.BlockSpec((1, tm, d), tile), pl.BlockSpec((1, tm, d), tile), pl.BlockSpec((1, tm, LANES), tile),
                   pl.BlockSpec((1, LANES), lambda b, i: (0, 0))],
        out_shape=[jax.ShapeDtypeStruct((bx, t, d), F32), jax.ShapeDtypeStruct((bx, t, d), F32),
                   jax.ShapeDtypeStruct((bx, t, LANES), F32), jax.ShapeDtypeStruct((1, LANES), F32)],
        compiler_params=_params(2),
        name=name,
    )(ya, yb, yc, g, x, gate, shift, scale, wa, wb, wc, wo, wr, tri, base)


def _dispatch_kernel(dest_ref, h_ref, xs_in_ref, xs_ref, sem, *, tokens):
    del xs_in_ref

    def row_copy(r, d):
        return pltpu.make_async_copy(h_ref.at[pl.ds(r, 1), :], xs_ref.at[pl.ds(d, 1), :], sem)

    def issue(r, carry):
        for k in range(2):
            row_copy(r, dest_ref[0, 0, 2 * r + k]).start()
        return carry

    lax.fori_loop(0, tokens, issue, 0, unroll=8)
    for _ in range(2):
        pltpu.make_async_copy(h_ref, xs_ref.at[pl.ds(0, tokens), :], sem).wait()


def _dispatch(dest, h, xs, *, td):
    n, d = h.shape
    return pl.pallas_call(
        functools.partial(_dispatch_kernel, tokens=td),
        grid=(n // td,),
        in_specs=[pl.BlockSpec((1, 1, 2 * td), lambda i: (i, 0, 0), memory_space=pltpu.SMEM),
                  pl.BlockSpec((td, d), lambda i: (i, 0)),
                  pl.BlockSpec(memory_space=pl.ANY)],
        out_specs=pl.BlockSpec(memory_space=pl.ANY),
        out_shape=jax.ShapeDtypeStruct(xs.shape, xs.dtype),
        scratch_shapes=[pltpu.SemaphoreType.DMA(())],
        input_output_aliases={2: 0},
        compiler_params=_params(1, row_dma=True),
        name="moe_dispatch",
    )(dest.reshape(n // td, 1, 2 * td), h, xs)


def _experts_kernel(te_ref, used_ref, xs_ref, wg_ref, wu_ref, wd_ref, y_ref, wg_bf, wu_bf, wd_bf):
    t = pl.program_id(0)
    live = t < used_ref[0]

    @pl.when(live & ((t == 0) | (te_ref[t] != te_ref[jnp.maximum(t - 1, 0)])))
    def _():
        wg_bf[...] = wg_ref[0, 0].astype(BF16)
        wu_bf[...] = wu_ref[0, 0].astype(BF16)
        wd_bf[...] = wd_ref[0, 0].astype(BF16)

    @pl.when(live)
    def _():
        xb = xs_ref[...].astype(BF16)
        a = jnp.dot(xb, wg_bf[...], preferred_element_type=F32)
        u = jnp.dot(xb, wu_bf[...], preferred_element_type=F32)
        y_ref[...] = jnp.dot((a * _sigmoid(a) * u).astype(BF16), wd_bf[...], preferred_element_type=F32)

    @pl.when(t >= used_ref[0])
    def _():
        y_ref[...] = jnp.zeros_like(y_ref)


def _experts(tile_expert, n_used, xs, wg, wu, wd, *, layer, tm):
    p, d = xs.shape
    hid = wg.shape[3]
    weights = lambda t, te, nu: (layer, te[t], 0, 0)
    return pl.pallas_call(
        _experts_kernel,
        grid_spec=pltpu.PrefetchScalarGridSpec(
            num_scalar_prefetch=2,
            grid=(p // tm,),
            in_specs=[pl.BlockSpec((tm, d), lambda t, te, nu: (t, 0)),
                      pl.BlockSpec((1, 1, d, hid), weights), pl.BlockSpec((1, 1, d, hid), weights),
                      pl.BlockSpec((1, 1, hid, d), weights)],
            out_specs=pl.BlockSpec((tm, d), lambda t, te, nu: (t, 0)),
            scratch_shapes=[pltpu.VMEM((d, hid), BF16), pltpu.VMEM((d, hid), BF16), pltpu.VMEM((hid, d), BF16)]),
        out_shape=jax.ShapeDtypeStruct((p, d), F32),
        compiler_params=_params(1),
        name="moe_experts",
    )(tile_expert, n_used, xs, wg, wu, wd)


def _combine_kernel(dest_ref, y_ref, x_ref, gate_ref, r_ref, o_ref, buf, sem, *, tokens):
    def row_copy(r, k, d):
        return pltpu.make_async_copy(y_ref.at[pl.ds(d, 1), :], buf.at[k, pl.ds(r, 1), :], sem)

    def issue(r, carry):
        for k in range(2):
            row_copy(r, k, dest_ref[0, 0, 2 * r + k]).start()
        return carry

    lax.fori_loop(0, tokens, issue, 0, unroll=8)
    for k in range(2):
        pltpu.make_async_copy(y_ref.at[pl.ds(0, tokens), :], buf.at[k], sem).wait()
    w = r_ref[...]
    o_ref[0] = x_ref[0] + gate_ref[0] * (w[:, 2:3] * buf[0] + w[:, 3:4] * buf[1])


def _combine(dest, y, x, gate, route, *, tc, name):
    bx, t, d = x.shape
    per = t // tc
    return pl.pallas_call(
        functools.partial(_combine_kernel, tokens=tc),
        grid=(bx * per,),
        in_specs=[pl.BlockSpec((1, 1, 2 * tc), lambda i: (i, 0, 0), memory_space=pltpu.SMEM),
                  pl.BlockSpec(memory_space=pl.ANY),
                  pl.BlockSpec((1, tc, d), lambda i: (i // per, i % per, 0)),
                  pl.BlockSpec((1, 1, d), lambda i: (i // per, 0, 0)),
                  pl.BlockSpec((tc, LANES), lambda i: (i, 0))],
        out_specs=pl.BlockSpec((1, tc, d), lambda i: (i // per, i % per, 0)),
        out_shape=jax.ShapeDtypeStruct(x.shape, F32),
        scratch_shapes=[pltpu.VMEM((2, tc, d), F32), pltpu.SemaphoreType.DMA(())],
        compiler_params=_params(1, row_dma=True),
        name=name,
    )(dest.reshape(bx * per, 1, 2 * tc), y, x, gate, route)


def _expert_slots(routes, counts, tm):
    n_assign = 2 * sum(r.shape[0] for r in routes)
    padded = ((counts + tm - 1) // tm) * tm
    ends = jnp.cumsum(padded)
    starts = ends - padded
    experts = jnp.arange(N_EXPERTS, dtype=jnp.int32)
    dests = []
    for r in routes:
        e = r[:, 0:2].astype(jnp.int32)
        rank = r[:, 4:6].astype(jnp.int32)
        start = jnp.sum(jnp.where(e[:, :, None] == experts, starts, 0), axis=-1)
        dests.append((start + rank).reshape(-1))
    n_tiles = n_assign // tm + N_EXPERTS
    tiles = jnp.arange(n_tiles, dtype=jnp.int32)
    tile_expert = jnp.minimum(jnp.sum((tiles[:, None] >= (ends // tm)[None, :]).astype(jnp.int32), axis=1),
                              N_EXPERTS - 1)
    return dests, tile_expert, (ends[-1:] // tm).astype(jnp.int32), n_tiles


def _rope_tables(length):
    pairs = HEAD_DIM // 4
    pos = jnp.arange(length, dtype=jnp.int32)
    row = (pos // GRID_W).astype(F32)
    col = (pos % GRID_W).astype(F32)
    freqs = ROPE_THETA ** (-jnp.arange(pairs, dtype=F32) / pairs)
    ang = jnp.concatenate([row[:, None] * freqs, col[:, None] * freqs], axis=-1)
    cos = jnp.tile(jnp.cos(ang), (1, 4))
    sin = jnp.tile(jnp.concatenate([-jnp.sin(ang), jnp.sin(ang)], axis=-1), (1, 2))
    return cos, sin


def kernel(x, c, ctx, c_ctx, w_ada, b_ada, w_in, a_qnorm, a_knorm, b_qnorm, b_knorm, c_qnorm, c_knorm, lambda_q1, lambda_k1, lambda_q2, lambda_k2, b_subln, c_sink, w_branch_a, w_branch_b, w_branch_c, w_out, w_router_group, w_router_expert, w_exp_gate, w_exp_up, w_exp_down):
    bsz, length, d = x.shape
    c_len = ctx.shape[1]
    depth = w_ada.shape[0]
    assert d == D_MODEL and bsz + 1 <= MOD_ROWS and length % 512 == 0 and c_len == 256

    cos, sin = _rope_tables(length)
    no_rope = jnp.zeros((c_len, LANES), F32)
    head_ids = jnp.arange(NORM_CHUNK, dtype=jnp.int32) // HEAD_DIM
    ones = (head_ids[:, None] == head_ids[None, :]).astype(BF16)
    c_rows = jnp.concatenate([c, c_ctx[None, :], jnp.zeros((MOD_ROWS - bsz - 1, d), F32)], axis=0)
    mod_all = _ada(c_rows, w_ada, b_ada.reshape(depth, 1, -1))
    zero_sink = jnp.zeros((8,), F32)

    tq_a, tq_b = 256, 256
    tk = 1024 if length % 1024 == 0 else 512
    n_lat = length // tk
    tm_moe = 256

    for l in range(depth):
        need_ctx = l < depth - 1
        mod = lambda k: mod_all[l, :bsz, k * d:(k + 1) * d].reshape(bsz, 1, d)
        cmod = lambda k: jnp.broadcast_to(mod_all[l, bsz, k * d:(k + 1) * d], (bsz, 1, d))
        lam_init = 0.8 - 0.6 * math.exp(-0.3 * l)
        lam = (jnp.exp(jnp.sum(lambda_q1[l] * lambda_k1[l])) - jnp.exp(jnp.sum(lambda_q2[l] * lambda_k2[l]))
               + lam_init).reshape(1).astype(F32)

        w = w_in[l]
        wqk = jnp.concatenate([w[:, 0:1536], w[:, 4864:5376], w[:, 4608:4736], w[:, 5888:6016]], axis=1).astype(BF16)
        wg = w[:, 1536:4608].astype(BF16)
        wv = jnp.concatenate([w[:, 5376:5888], w[:, 4736:4864], w[:, 6016:6144]], axis=1).astype(BF16)
        q_scale = HEAD_DIM ** -0.5 * LOG2E
        gain = jnp.concatenate([jnp.tile(a_qnorm[l], 8) * q_scale, jnp.tile(b_qnorm[l], 8) * q_scale,
                                jnp.tile(c_qnorm[l], 8) * q_scale, jnp.tile(b_knorm[l], 8),
                                jnp.tile(a_knorm[l], 2), jnp.tile(c_knorm[l], 2)]).reshape(1, QK_COLS)
        subln = b_subln[l].reshape(1, LANES)
        sink = c_sink[l] * LOG2E
        w_router = jnp.concatenate([w_router_group[l], w_router_expert[l],
                                    jnp.zeros((d, LANES - N_GROUPS - N_EXPERTS), F32)], axis=1)
        w_router_hi = w_router.astype(BF16)
        w_router_lo = (w_router - w_router_hi.astype(F32)).astype(BF16)
        merge_w = (w_branch_a[l].astype(BF16), w_branch_b[l].astype(BF16), w_branch_c[l].astype(BF16),
                   w_out[l].astype(BF16), jnp.concatenate([w_router_hi, w_router_lo], axis=1))

        qk, gates, v = _inproj(x, mod(0), mod(1), wqk, wg, wv, gain, ones, cos, sin, use_rope=True, tm=512)
        cqk, cgates, cv = _inproj(ctx, cmod(0), cmod(1), wqk, wg, wv, gain, ones, no_rope, no_rope,
                                  use_rope=False, tm=c_len)
        ya = _gqa(zero_sink, qk, qk, v, cqk, cv, q_blk=0, k_blk=16, v_blk=4, tq=tq_a, tk=tk, n_lat=n_lat,
                  use_sink=False, name="attn_gqa")
        yb = _diff(lam, qk, qk, v, cqk, cv, subln, tq=tq_b, tk=tk, n_lat=n_lat, out_scale=1.0 - lam_init,
                   name="attn_diff")
        yc = _window(sink, qk, qk, v, cqk, cv, tq=128, nb=4)
        x, h2, route, counts = _merge(ya, yb, yc, gates, x, mod(2), mod(3), mod(4), *merge_w,
                                      jnp.zeros((1, LANES), F32), tm=512, name="merge")
        tokens = [h2.reshape(bsz * length, d)]
        routes = [route.reshape(bsz * length, LANES)]
        if need_ctx:
            cya = _gqa(zero_sink, cqk, cqk, cv, cqk, cv, q_blk=0, k_blk=16, v_blk=4, tq=128, tk=tk, n_lat=0,
                       use_sink=False, name="ctx_gqa")
            cyb = _diff(lam, cqk, cqk, cv, cqk, cv, subln, tq=c_len, tk=tk, n_lat=0, out_scale=1.0 - lam_init,
                        name="ctx_diff")
            cyc = _gqa(sink, cqk, cqk, cv, cqk, cv, q_blk=2, k_blk=17, v_blk=5, tq=128, tk=tk, n_lat=0,
                       use_sink=True, name="ctx_sink")
            ctx, hc2, croute, counts = _merge(cya, cyb, cyc, cgates, ctx, cmod(2), cmod(3), cmod(4), *merge_w, counts,
                                              tm=c_len, name="ctx_merge")
            tokens.append(hc2.reshape(bsz * c_len, d))
            routes.append(croute.reshape(bsz * c_len, LANES))

        dests, tile_expert, n_used, n_tiles = _expert_slots(routes, counts[0, :N_EXPERTS].astype(jnp.int32), tm_moe)
        xs = jnp.zeros((n_tiles * tm_moe, d), F32)
        for t, dest in zip(tokens, dests):
            xs = _dispatch(dest, t, xs, td=256)
        y = _experts(tile_expert, n_used, xs, w_exp_gate, w_exp_up, w_exp_down, layer=l, tm=tm_moe)
        x = _combine(dests[0], y, x, mod(5), routes[0], tc=256, name="moe_combine")
        if need_ctx:
            ctx = _combine(dests[1], y, ctx, cmod(5), routes[1], tc=256, name="ctx_combine")
    return x
```

```python
import functools
import math

import jax
import jax.numpy as jnp
from jax import lax
from jax.experimental import pallas as pl
from jax.experimental.pallas import tpu as pltpu

F32 = jnp.float32
BF16 = jnp.bfloat16

D_MODEL = 1024
HEAD_DIM = 64
GRID_W = 64
ROPE_THETA = 10000.0
WINDOW = 128
N_GROUPS = 4
EXPERTS_PER_GROUP = 8
N_EXPERTS = N_GROUPS * EXPERTS_PER_GROUP
EXPERT_HIDDEN = D_MODEL // 2
N_MOD = 6
EPS = 1e-6
NEG = -1e30
LOG2E = math.log2(math.e)
LANES = 128
MOD_ROWS = 16

QK_COLS = 2304
G_COLS = 3 * D_MODEL
V_COLS = 768
NORM_CHUNK = 256
SUB = 256

VMEM_LIMIT = 48 * 1024 * 1024


def _params(n_axes, vmem=VMEM_LIMIT, row_dma=False):
    return pltpu.CompilerParams(dimension_semantics=("arbitrary",) * n_axes, vmem_limit_bytes=vmem,
                                disable_bounds_checks=row_dma)


def _resident(shape):
    return pl.BlockSpec(shape, lambda *_: (0,) * len(shape), pipeline_mode=pl.Buffered(1))


def _sigmoid(v):
    return 1.0 / (1.0 + jnp.exp(-v))


def _rms_modulate(v, shift, scale):
    v = v * lax.rsqrt(jnp.mean(v * v, axis=-1, keepdims=True) + EPS)
    return v * (1.0 + scale) + shift


def _ada_kernel(c_ref, w_ref, b_ref, o_ref):
    c = c_ref[...]
    o_ref[0] = jnp.dot(c * _sigmoid(c), w_ref[0], preferred_element_type=F32,
                       precision=lax.Precision.HIGHEST) + b_ref[0]


def _ada(c_rows, w_ada, b_ada):
    depth, d, n = w_ada.shape
    tn = 1536
    return pl.pallas_call(
        _ada_kernel,
        grid=(depth, n // tn),
        in_specs=[pl.BlockSpec((MOD_ROWS, d), lambda l, j: (0, 0)),
                  pl.BlockSpec((1, d, tn), lambda l, j: (l, 0, j)),
                  pl.BlockSpec((1, 1, tn), lambda l, j: (l, 0, j))],
        out_specs=pl.BlockSpec((1, MOD_ROWS, tn), lambda l, j: (l, 0, j)),
        out_shape=jax.ShapeDtypeStruct((depth, MOD_ROWS, n), F32),
        compiler_params=_params(2),
        name="ada",
    )(c_rows, w_ada, b_ada)


def _rope_partner(z):
    lane = lax.broadcasted_iota(jnp.int32, z.shape, 1)
    return jnp.where((lane & 32) == 0, pltpu.roll(z, 96, 1), pltpu.roll(z, 32, 1))


def _inproj_kernel(x_ref, shift_ref, scale_ref, wqk_ref, wg_ref, wv_ref, gain_ref, ones_ref, cos_ref, sin_ref,
                   oqk_ref, og_ref, ov_ref, *, use_rope):
    hb = _rms_modulate(x_ref[0], shift_ref[0], scale_ref[0]).astype(BF16)
    chunks = [slice(c * NORM_CHUNK, (c + 1) * NORM_CHUNK) for c in range(QK_COLS // NORM_CHUNK)]
    ys = [jnp.dot(hb, wqk_ref[:, cols], preferred_element_type=F32) for cols in chunks]
    sss = [jnp.dot((y * y).astype(BF16), ones_ref[...], preferred_element_type=F32) for y in ys]
    for c in range(G_COLS // 512):
        cols = slice(c * 512, (c + 1) * 512)
        og_ref[0, :, cols] = _sigmoid(jnp.dot(hb, wg_ref[:, cols], preferred_element_type=F32)).astype(BF16)
    ov_ref[0] = jnp.dot(hb, wv_ref[...], preferred_element_type=F32).astype(BF16)
    for cols, y, ss in zip(chunks, ys, sss):
        yn = y * lax.rsqrt(ss * (1.0 / HEAD_DIM) + EPS) * gain_ref[:, cols]
        if use_rope:
            halves = []
            for k in range(NORM_CHUNK // LANES):
                z = yn[:, k * LANES:(k + 1) * LANES]
                halves.append(z * cos_ref[...] + _rope_partner(z) * sin_ref[...])
            yn = jnp.concatenate(halves, axis=1)
        oqk_ref[0, :, cols] = yn.astype(BF16)


def _inproj(x, shift, scale, wqk, wg, wv, gain, ones, cos, sin, *, use_rope, tm):
    bx, t, d = x.shape
    row = lambda b, i: (b, 0, 0)
    tile = lambda b, i: (b, i, 0)
    return pl.pallas_call(
        functools.partial(_inproj_kernel, use_rope=use_rope),
        grid=(bx, t // tm),
        in_specs=[pl.BlockSpec((1, tm, d), tile),
                  pl.BlockSpec((1, 1, d), row), pl.BlockSpec((1, 1, d), row),
                  _resident((d, QK_COLS)), _resident((d, G_COLS)), _resident((d, V_COLS)),
                  _resident((1, QK_COLS)), _resident((NORM_CHUNK, NORM_CHUNK)),
                  pl.BlockSpec((tm, LANES), lambda b, i: (i, 0)), pl.BlockSpec((tm, LANES), lambda b, i: (i, 0))],
        out_specs=[pl.BlockSpec((1, tm, QK_COLS), tile), pl.BlockSpec((1, tm, G_COLS), tile),
                   pl.BlockSpec((1, tm, V_COLS), tile)],
        out_shape=[jax.ShapeDtypeStruct((bx, t, QK_COLS), BF16), jax.ShapeDtypeStruct((bx, t, G_COLS), BF16),
                   jax.ShapeDtypeStruct((bx, t, V_COLS), BF16)],
        compiler_params=_params(2),
        name="inproj_rope" if use_rope else "inproj_ctx",
    )(x, shift, scale, wqk, wg, wv, gain, ones, cos, sin)


def _qk(qs, k):
    return lax.dot_general(qs, k, (((1,), (1,)), ((), ())), preferred_element_type=F32)


def _online_softmax(streams, state, kl_ref, vl_ref, kc_ref, vc_ref, n_lat, tk):
    n = len(streams)
    state = [state[4 * i:4 * i + 4] for i in range(n)]
    for (qs, _, m0, l0), (qs_ref, m_ref, l_ref, acc_ref) in zip(streams, state):
        lane = lax.broadcasted_iota(jnp.int32, m_ref.shape, 1)
        qs_ref[...] = qs
        m_ref[...] = jnp.broadcast_to(m0, m_ref.shape)
        l_ref[...] = jnp.where(lane == 0, l0, 0.0)
        acc_ref[...] = jnp.zeros(acc_ref.shape, F32)

    def scores(i, key_block, width):
        qs = state[i][0][...]
        return [_qk(qs, key_block(c)) for c in range(width // SUB)]

    def update(i, subs, value_block):
        _, m_ref, l_ref, acc_ref = state[i]
        m = m_ref[...]
        top = subs[0]
        for s in subs[1:]:
            top = jnp.maximum(top, s)
        top = functools.reduce(jnp.maximum, [top[:, k * LANES:(k + 1) * LANES] for k in range(SUB // LANES)])
        m_new = jnp.maximum(m, jnp.max(top, axis=-1, keepdims=True))
        alpha = jnp.exp2(m - m_new)
        m_wide = jnp.concatenate([m_new] * (SUB // LANES), axis=1)
        l = alpha * l_ref[...]
        acc = alpha * acc_ref[...]
        for c, s in enumerate(subs):
            p = jnp.exp2(s - m_wide)
            for k in range(SUB // LANES):
                l = l + p[:, k * LANES:(k + 1) * LANES]
            acc = acc + jnp.dot(p.astype(BF16), value_block(c), preferred_element_type=F32)
        m_ref[...] = m_new
        l_ref[...] = l
        acc_ref[...] = acc

    def step(key_block, value_block, width):
        subs = [scores(i, functools.partial(key_block, lanes), width) for i, (_, lanes, _, _) in enumerate(streams)]
        for i, (_, lanes, _, _) in enumerate(streams):
            update(i, subs[i], functools.partial(value_block, lanes))

    if n_lat:
        def body(j, carry):
            rows = lambda c: pl.ds(pl.multiple_of(j * tk + c * SUB, SUB), SUB)
            step(lambda lanes, c: kl_ref[0, rows(c), lanes], lambda lanes, c: vl_ref[0, rows(c), lanes], tk)
            return carry
        lax.fori_loop(0, n_lat, body, 0)
    step(lambda lanes, c: kc_ref[0, c * SUB:(c + 1) * SUB, lanes],
         lambda lanes, c: vc_ref[0, c * SUB:(c + 1) * SUB, lanes], kc_ref.shape[1])
    return [acc_ref[...] / jnp.sum(l_ref[...], axis=-1, keepdims=True) for _, _, l_ref, acc_ref in state]


def _softmax_state(n_streams, rows):
    return [pltpu.VMEM((rows, LANES), BF16), pltpu.VMEM((rows, LANES), F32), pltpu.VMEM((rows, LANES), F32),
            pltpu.VMEM((rows, LANES), F32)] * n_streams


def _stack_pair(q_ref, c, tq, row0=0):
    g = c // 2
    lane = lax.broadcasted_iota(jnp.int32, (tq, LANES), 1)
    keep = (lane >= HEAD_DIM) if g else (lane < HEAD_DIM)
    z = q_ref[0, row0:row0 + tq, c * LANES:(c + 1) * LANES].astype(F32)
    swapped = pltpu.roll(z, HEAD_DIM, 1)
    first, second = (z, swapped) if g == 0 else (swapped, z)
    return jnp.concatenate([jnp.where(keep, first, 0.0).astype(BF16), jnp.where(keep, second, 0.0).astype(BF16)], axis=0)


def _unstack_pair(o, c, tq):
    lane = lax.broadcasted_iota(jnp.int32, (tq, LANES), 1)
    a, b = o[:tq], o[tq:]
    if c // 2 == 0:
        b = pltpu.roll(b, HEAD_DIM, 1)
    else:
        a = pltpu.roll(a, HEAD_DIM, 1)
    return jnp.where(lane < HEAD_DIM, a, b)


def _sink_rows(sink_ref, heads, tq):
    return jnp.concatenate([jnp.full((tq, 1), sink_ref[h], F32) for h in heads], axis=0)


def _gqa_kernel(sink_ref, q_ref, kl_ref, vl_ref, kc_ref, vc_ref, o_ref, *state, tq, tk, n_lat, use_sink):
    streams = []
    for c in range(4):
        if use_sink:
            m0, l0 = _sink_rows(sink_ref, (2 * c, 2 * c + 1), tq), jnp.ones((2 * tq, 1), F32)
        else:
            m0, l0 = jnp.full((2 * tq, 1), NEG, F32), jnp.zeros((2 * tq, 1), F32)
        streams.append((_stack_pair(q_ref, c, tq), slice(0, LANES), m0, l0))
    outs = _online_softmax(streams, state, kl_ref, vl_ref, kc_ref, vc_ref, n_lat, tk)
    for c, o in enumerate(outs):
        o_ref[0, :, c * LANES:(c + 1) * LANES] = _unstack_pair(o, c, tq).astype(BF16)


def _diff_kernel(lam_ref, q_ref, kl_ref, vl_ref, kc_ref, vc_ref, gain_ref, o_ref, *state, tq, tk, n_lat, out_scale):
    lane = lax.broadcasted_iota(jnp.int32, (tq, LANES), 1)
    streams = []
    for n in range(4):
        lanes = slice(n * LANES, (n + 1) * LANES)
        z = q_ref[0, :, lanes]
        zero = jnp.zeros_like(z)
        qs = jnp.concatenate([jnp.where(lane < HEAD_DIM, z, zero), jnp.where(lane >= HEAD_DIM, z, zero)], axis=0)
        streams.append((qs, lanes, jnp.full((2 * tq, 1), NEG, F32), jnp.zeros((2 * tq, 1), F32)))
    outs = _online_softmax(streams, state, kl_ref, vl_ref, kc_ref, vc_ref, n_lat, tk)
    for n, o in enumerate(outs):
        d = o[:tq] - lam_ref[0] * o[tq:]
        d = d * lax.rsqrt(jnp.mean(d * d, axis=-1, keepdims=True) + EPS) * gain_ref[...] * out_scale
        o_ref[0, :, n * LANES:(n + 1) * LANES] = d.astype(BF16)


def _lane_blocks(a):
    return [a[:, k * LANES:(k + 1) * LANES] for k in range(a.shape[1] // LANES)]


def _window_kernel(sink_ref, q_ref, kl_ref, vl_ref, kc_ref, vc_ref, o_ref, *, tq, nb, length):
    span = tq + 2 * WINDOW
    kc = kc_ref[0]
    vc = vc_ref[0]
    rel = lax.broadcasted_iota(jnp.int32, (tq, span), 1) - lax.broadcasted_iota(jnp.int32, (tq, span), 0)
    chains = [(j, c) for j in range(nb) for c in range(4)]
    starts = []
    for j in range(nb):
        first = (pl.program_id(1) * nb + j) * tq
        starts.append((first, pl.multiple_of(jnp.clip(first - WINDOW, 0, length - span), LANES)))
    sw, sc = [], []
    for j, c in chains:
        first, start = starts[j]
        valid = jnp.abs(rel + (start - first)) <= WINDOW
        qs = _stack_pair(q_ref, c, tq, j * tq)
        sw.append(jnp.where(valid[None], _qk(qs, kl_ref[0, pl.ds(start, span), :]).reshape(2, tq, span),
                            NEG).reshape(2 * tq, span))
        sc.append(_qk(qs, kc))
    snk = [_sink_rows(sink_ref, (2 * c, 2 * c + 1), tq) for _, c in chains]
    top = [jnp.max(functools.reduce(jnp.maximum, _lane_blocks(a) + _lane_blocks(b)), axis=-1, keepdims=True)
           for a, b in zip(sw, sc)]
    m = [jnp.maximum(t, s) for t, s in zip(top, snk)]
    pw = [jnp.exp2(a - mm) for a, mm in zip(sw, m)]
    pc = [jnp.exp2(b - mm) for b, mm in zip(sc, m)]
    tot = [jnp.sum(functools.reduce(jnp.add, _lane_blocks(a) + _lane_blocks(b)), axis=-1, keepdims=True)
           for a, b in zip(pw, pc)]
    l = [t + jnp.exp2(s - mm) for t, s, mm in zip(tot, snk, m)]
    o = [jnp.dot(a.astype(BF16), vl_ref[0, pl.ds(starts[j][1], span), :], preferred_element_type=F32)
         + jnp.dot(b.astype(BF16), vc, preferred_element_type=F32) for (j, _), a, b in zip(chains, pw, pc)]
    o = [a / ll for a, ll in zip(o, l)]
    for (j, c), a in zip(chains, o):
        o_ref[0, j * tq:(j + 1) * tq, c * LANES:(c + 1) * LANES] = _unstack_pair(a, c, tq).astype(BF16)


_SMEM = pl.BlockSpec(memory_space=pltpu.SMEM)


def _kv_specs(klat, kctx, k_blk, v_blk, width):
    s_lat, s_ctx = klat.shape[1], kctx.shape[1]
    return [pl.BlockSpec((1, s_lat, width), lambda b, i: (b, 0, k_blk)),
            pl.BlockSpec((1, s_lat, width), lambda b, i: (b, 0, v_blk)),
            pl.BlockSpec((1, s_ctx, width), lambda b, i: (b, 0, k_blk)),
            pl.BlockSpec((1, s_ctx, width), lambda b, i: (b, 0, v_blk))]


def _gqa(sink, q, klat, vlat, kctx, vctx, *, q_blk, k_blk, v_blk, tq, tk, n_lat, use_sink, name):
    bx, t, _ = q.shape
    return pl.pallas_call(
        functools.partial(_gqa_kernel, tq=tq, tk=tk, n_lat=n_lat, use_sink=use_sink),
        grid=(bx, t // tq),
        in_specs=[_SMEM, pl.BlockSpec((1, tq, 512), lambda b, i: (b, i, q_blk))]
        + _kv_specs(klat, kctx, k_blk, v_blk, LANES),
        out_specs=pl.BlockSpec((1, tq, 512), lambda b, i: (b, i, 0)),
        out_shape=jax.ShapeDtypeStruct((bx, t, 512), BF16),
        scratch_shapes=_softmax_state(4, 2 * tq),
        compiler_params=_params(2),
        name=name,
    )(sink, q, klat, vlat, kctx, vctx)


def _diff(lam, q, klat, vlat, kctx, vctx, gain, *, tq, tk, n_lat, out_scale, name):
    bx, t, _ = q.shape
    return pl.pallas_call(
        functools.partial(_diff_kernel, tq=tq, tk=tk, n_lat=n_lat, out_scale=out_scale),
        grid=(bx, t // tq),
        in_specs=[_SMEM, pl.BlockSpec((1, tq, 512), lambda b, i: (b, i, 1))]
        + _kv_specs(klat, kctx, 3, 0, 512) + [pl.BlockSpec((1, LANES), lambda b, i: (0, 0))],
        out_specs=pl.BlockSpec((1, tq, 512), lambda b, i: (b, i, 0)),
        out_shape=jax.ShapeDtypeStruct((bx, t, 512), BF16),
        scratch_shapes=_softmax_state(4, 2 * tq),
        compiler_params=_params(2),
        name=name,
    )(lam, q, klat, vlat, kctx, vctx, gain)


def _window(sink, q, klat, vlat, kctx, vctx, *, tq, nb):
    bx, t, _ = q.shape
    return pl.pallas_call(
        functools.partial(_window_kernel, tq=tq, nb=nb, length=t),
        grid=(bx, t // (nb * tq)),
        in_specs=[_SMEM, pl.BlockSpec((1, nb * tq, 512), lambda b, i: (b, i, 2))]
        + _kv_specs(klat, kctx, 17, 5, LANES),
        out_specs=pl.BlockSpec((1, nb * tq, 512), lambda b, i: (b, i, 0)),
        out_shape=jax.ShapeDtypeStruct((bx, t, 512), BF16),
        compiler_params=_params(2),
        name="attn_window",
    )(sink, q, klat, vlat, kctx, vctx)


def _route(logits, tri, base):
    lane = lax.broadcasted_iota(jnp.int32, logits.shape, 1).astype(F32)
    first = lambda hit: jnp.min(jnp.where(hit, lane, float(LANES)), axis=-1, keepdims=True)
    gl = jnp.where(lane < N_GROUPS, logits, NEG)
    gmax = jnp.max(gl, axis=-1, keepdims=True)
    gidx = first(gl == gmax)
    gw = 1.0 / jnp.sum(jnp.exp(gl - gmax), axis=-1, keepdims=True)
    lo = N_GROUPS + gidx * EXPERTS_PER_GROUP
    el = jnp.where((lane >= lo) & (lane < lo + EXPERTS_PER_GROUP), logits, NEG)
    v1 = jnp.max(el, axis=-1, keepdims=True)
    i1 = first(el == v1)
    el = jnp.where(lane == i1, NEG, el)
    v2 = jnp.max(el, axis=-1, keepdims=True)
    i2 = first(el == v2)
    e = jnp.exp(v2 - v1)
    w1 = gw / (1.0 + e)
    w2 = gw * e / (1.0 + e)
    e1 = i1 - N_GROUPS
    e2 = i2 - N_GROUPS
    hit1 = jnp.where(lane == e1, 1.0, 0.0)
    hit2 = jnp.where(lane == e2, 1.0, 0.0)
    before1 = jnp.dot(tri, hit1.astype(BF16), preferred_element_type=F32) + base
    base = base + jnp.sum(hit1, axis=0, keepdims=True)
    before2 = jnp.dot(tri, hit2.astype(BF16), preferred_element_type=F32) + base
    base = base + jnp.sum(hit2, axis=0, keepdims=True)
    r1 = jnp.sum(hit1 * before1, axis=-1, keepdims=True)
    r2 = jnp.sum(hit2 * before2, axis=-1, keepdims=True)
    out = jnp.zeros_like(logits)
    for k, val in enumerate((e1, e2, w1, w2, r1, r2)):
        out = jnp.where(lane == k, val, out)
    return out, base


def _merge_kernel(ya_ref, yb_ref, yc_ref, g_ref, x_ref, gate_ref, shift_ref, scale_ref,
                  wa_ref, wb_ref, wc_ref, wo_ref, wr_ref, tri_ref, base_ref, xo_ref, h_ref, r_ref, cnt_ref):
    d = D_MODEL

    @pl.when((pl.program_id(0) == 0) & (pl.program_id(1) == 0))
    def _():
        cnt_ref[...] = base_ref[...]

    m = None
    for k, (y_ref, w_ref) in enumerate(((ya_ref, wa_ref), (yb_ref, wb_ref), (yc_ref, wc_ref))):
        t = g_ref[0, :, k * d:(k + 1) * d].astype(F32) * jnp.dot(y_ref[0], w_ref[...], preferred_element_type=F32)
        m = t if m is None else m + t
    xn = x_ref[0] + gate_ref[0] * jnp.dot(m.astype(BF16), wo_ref[...], preferred_element_type=F32)
    xo_ref[0] = xn
    h2 = _rms_modulate(xn, shift_ref[0], scale_ref[0])
    h_ref[0] = h2
    hi = h2.astype(BF16)
    lo = (h2 - hi.astype(F32)).astype(BF16)
    both = jnp.dot(hi, wr_ref[...], preferred_element_type=F32)
    logits = (both[:, :LANES] + both[:, LANES:]) + jnp.dot(lo, wr_ref[:, :LANES], preferred_element_type=F32)
    r_ref[0], cnt_ref[...] = _route(logits, tri_ref[...], cnt_ref[...])


def _merge(ya, yb, yc, g, x, gate, shift, scale, wa, wb, wc, wo, wr, base, *, tm, name):
    bx, t, d = x.shape
    row = lambda b, i: (b, 0, 0)
    tile = lambda b, i: (b, i, 0)
    ids = jnp.arange(tm, dtype=jnp.int32)
    tri = (ids[None, :] < ids[:, None]).astype(BF16)
    return pl.pallas_call(
        _merge_kernel,
        grid=(bx, t // tm),
        in_specs=[pl.BlockSpec((1, tm, 512), tile)] * 3
        + [pl.BlockSpec((1, tm, G_COLS), tile), pl.BlockSpec((1, tm, d), tile)]
        + [pl.BlockSpec((1, 1, d), row)] * 3
        + [_resident((512, d))] * 3 + [_resident((d, d)), _resident((d, 2 * LANES)), _resident((tm, tm)),
                                      _resident((1, LANES))],
        out_specs=[pl.BlockSpec((1, tm, d), tile), pl.BlockSpec((1, tm, d), tile), pl.BlockSpec((1, tm, LANES), tile),
                   pl.BlockSpec((1, LANES), lambda b, i: (0, 0))],
        out_shape=[jax.ShapeDtypeStruct((bx, t, d), F32), jax.ShapeDtypeStruct((bx, t, d), F32),
                   jax.ShapeDtypeStruct((bx, t, LANES), F32), jax.ShapeDtypeStruct((1, LANES), F32)],
        compiler_params=_params(2),
        name=name,
    )(ya, yb, yc, g, x, gate, shift, scale, wa, wb, wc, wo, wr, tri, base)


def _experts_kernel(te_ref, gidx_ref, tgt_ref, h_ref, wg_ref, wu_ref, wd_ref, y_ref,
                    x0, x1, y0, y1, wg_bf, wu_bf, wd_bf, gsem, ssem, *, tm, n_tiles):
    s = pl.program_id(0)
    xbuf, ybuf = (x0, x1), (y0, y1)

    def gather(p):
        for r in range(tm):
            pltpu.make_async_copy(h_ref.at[pl.ds(gidx_ref[0, 0, r], 1), :], xbuf[p].at[pl.ds(r, 1), :],
                                  gsem.at[p]).start()

    def scatter(p):
        for r in range(tm):
            pltpu.make_async_copy(ybuf[p].at[pl.ds(r, 1), :], y_ref.at[pl.ds(tgt_ref[0, 0, r], 1), :],
                                  ssem.at[p]).start()

    def compute(p):
        xb = xbuf[p][...].astype(BF16)
        a = jnp.dot(xb, wg_bf[...], preferred_element_type=F32)
        u = jnp.dot(xb, wu_bf[...], preferred_element_type=F32)
        ybuf[p][...] = jnp.dot((a * _sigmoid(a) * u).astype(BF16), wd_bf[...], preferred_element_type=F32)

    def wait_gather(p):
        pltpu.make_async_copy(h_ref.at[pl.ds(0, tm), :], xbuf[p], gsem.at[p]).wait()

    def wait_scatter(p):
        pltpu.make_async_copy(ybuf[p], y_ref.at[pl.ds(0, tm), :], ssem.at[p]).wait()

    tile = jnp.clip(s - 1, 0, n_tiles - 1)
    @pl.when((s >= 1) & (s <= n_tiles) & ((s == 1) | (te_ref[tile] != te_ref[jnp.maximum(tile - 1, 0)])))
    def _():
        wg_bf[...] = wg_ref[0, 0].astype(BF16)
        wu_bf[...] = wu_ref[0, 0].astype(BF16)
        wd_bf[...] = wd_ref[0, 0].astype(BF16)

    for p in (0, 1):
        q = 1 - p

        @pl.when(s % 2 == p)
        def _():
            @pl.when((s >= 1) & (s <= n_tiles))
            def _():
                wait_gather(q)

            @pl.when((s >= 3) & (s <= n_tiles + 2))
            def _():
                wait_scatter(q)

            steady = (s >= 2) & (s < n_tiles)

            @pl.when(steady)
            def _():
                gather(p)
                compute(q)
                scatter(p)

            @pl.when(jnp.logical_not(steady))
            def _():
                @pl.when(s < n_tiles)
                def _():
                    gather(p)

                @pl.when((s >= 1) & (s <= n_tiles))
                def _():
                    compute(q)

                @pl.when((s >= 2) & (s <= n_tiles + 1))
                def _():
                    scatter(p)


def _experts(tile_expert, gidx, tgt, h, wg, wu, wd, *, layer, tm, out_rows):
    n_tiles = gidx.shape[0]
    d = h.shape[1]
    hid = wg.shape[3]
    weights = lambda s, te: (layer, te[jnp.clip(s - 1, 0, n_tiles - 1)], 0, 0)
    smem_block = lambda shift: pl.BlockSpec((1, 1, tm), lambda s, te: (jnp.clip(s - shift, 0, n_tiles - 1), 0, 0),
                                            memory_space=pltpu.SMEM)
    return pl.pallas_call(
        functools.partial(_experts_kernel, tm=tm, n_tiles=n_tiles),
        grid_spec=pltpu.PrefetchScalarGridSpec(
            num_scalar_prefetch=1,
            grid=(n_tiles + 3,),
            in_specs=[smem_block(0), smem_block(2), pl.BlockSpec(memory_space=pl.ANY),
                      pl.BlockSpec((1, 1, d, hid), weights), pl.BlockSpec((1, 1, d, hid), weights),
                      pl.BlockSpec((1, 1, hid, d), weights)],
            out_specs=pl.BlockSpec(memory_space=pl.ANY),
            scratch_shapes=[pltpu.VMEM((tm, d), F32)] * 4
            + [pltpu.VMEM((d, hid), BF16), pltpu.VMEM((d, hid), BF16), pltpu.VMEM((hid, d), BF16),
               pltpu.SemaphoreType.DMA((2,)), pltpu.SemaphoreType.DMA((2,))]),
        out_shape=jax.ShapeDtypeStruct((out_rows, d), F32),
        compiler_params=_params(1, row_dma=True),
        name="moe_experts",
    )(tile_expert, gidx, tgt, h, wg, wu, wd)


def _combine_kernel(y_ref, x_ref, gate_ref, r_ref, o_ref):
    d = x_ref.shape[2]
    w = r_ref[...]
    o_ref[0] = x_ref[0] + gate_ref[0] * (w[:, 2:3] * y_ref[:, :d] + w[:, 3:4] * y_ref[:, d:])


def _combine(y, x, gate, route, *, first_token, tc, name):
    bx, t, d = x.shape
    per = t // tc
    return pl.pallas_call(
        _combine_kernel,
        grid=(bx * per,),
        in_specs=[pl.BlockSpec((tc, 2 * d), lambda i: (first_token // tc + i, 0)),
                  pl.BlockSpec((1, tc, d), lambda i: (i // per, i % per, 0)),
                  pl.BlockSpec((1, 1, d), lambda i: (i // per, 0, 0)),
                  pl.BlockSpec((tc, LANES), lambda i: (i, 0))],
        out_specs=pl.BlockSpec((1, tc, d), lambda i: (i // per, i % per, 0)),
        out_shape=jax.ShapeDtypeStruct(x.shape, F32),
        compiler_params=_params(1),
        name=name,
    )(y, x, gate, route)


def _expert_slots(routes, counts, tm):
    n_assign = 2 * routes.shape[0]
    padded = ((counts + tm - 1) // tm) * tm
    ends = jnp.cumsum(padded)
    starts = ends - padded
    experts = jnp.arange(N_EXPERTS, dtype=jnp.int32)
    e = routes[:, 0:2].astype(jnp.int32)
    rank = routes[:, 4:6].astype(jnp.int32)
    dest = (jnp.sum(jnp.where(e[:, :, None] == experts, starts, 0), axis=-1) + rank).reshape(-1)
    n_tiles = n_assign // tm + N_EXPERTS
    n_slots = n_tiles * tm
    src = jnp.full((n_slots,), -1, jnp.int32).at[dest].set(jnp.arange(n_assign, dtype=jnp.int32),
                                                           unique_indices=True, mode="promise_in_bounds")
    gidx = jnp.maximum(src, 0) // 2
    tgt = jnp.where(src >= 0, src, n_assign - 1 + jnp.cumsum((src < 0).astype(jnp.int32)))
    tiles = jnp.arange(n_tiles, dtype=jnp.int32)
    tile_expert = jnp.minimum(jnp.sum((tiles[:, None] >= (ends // tm)[None, :]).astype(jnp.int32), axis=1),
                              N_EXPERTS - 1)
    return gidx.reshape(n_tiles, 1, tm), tgt.reshape(n_tiles, 1, tm), tile_expert, n_slots


def _rope_tables(length):
    pairs = HEAD_DIM // 4
    pos = jnp.arange(length, dtype=jnp.int32)
    row = (pos // GRID_W).astype(F32)
    col = (pos % GRID_W).astype(F32)
    freqs = ROPE_THETA ** (-jnp.arange(pairs, dtype=F32) / pairs)
    ang = jnp.concatenate([row[:, None] * freqs, col[:, None] * freqs], axis=-1)
    cos = jnp.tile(jnp.cos(ang), (1, 4))
    sin = jnp.tile(jnp.concatenate([-jnp.sin(ang), jnp.sin(ang)], axis=-1), (1, 2))
    return cos, sin


def kernel(x, c, ctx, c_ctx, w_ada, b_ada, w_in, a_qnorm, a_knorm, b_qnorm, b_knorm, c_qnorm, c_knorm, lambda_q1, lambda_k1, lambda_q2, lambda_k2, b_subln, c_sink, w_branch_a, w_branch_b, w_branch_c, w_out, w_router_group, w_router_expert, w_exp_gate, w_exp_up, w_exp_down):
    bsz, length, d = x.shape
    c_len = ctx.shape[1]
    depth = w_ada.shape[0]
    assert d == D_MODEL and bsz + 1 <= MOD_ROWS and length % 512 == 0 and c_len == 256

    cos, sin = _rope_tables(length)
    no_rope = jnp.zeros((c_len, LANES), F32)
    head_ids = jnp.arange(NORM_CHUNK, dtype=jnp.int32) // HEAD_DIM
    ones = (head_ids[:, None] == head_ids[None, :]).astype(BF16)
    c_rows = jnp.concatenate([c, c_ctx[None, :], jnp.zeros((MOD_ROWS - bsz - 1, d), F32)], axis=0)
    mod_all = _ada(c_rows, w_ada, b_ada.reshape(depth, 1, -1))
    zero_sink = jnp.zeros((8,), F32)

    tq_a, tq_b = 256, 256
    tk = 2048 if length % 2048 == 0 else 512
    n_lat = length // tk
    tm_moe = 256

    for l in range(depth):
        need_ctx = l < depth - 1
        mod = lambda k: mod_all[l, :bsz, k * d:(k + 1) * d].reshape(bsz, 1, d)
        cmod = lambda k: jnp.broadcast_to(mod_all[l, bsz, k * d:(k + 1) * d], (bsz, 1, d))
        lam_init = 0.8 - 0.6 * math.exp(-0.3 * l)
        lam = (jnp.exp(jnp.sum(lambda_q1[l] * lambda_k1[l])) - jnp.exp(jnp.sum(lambda_q2[l] * lambda_k2[l]))
               + lam_init).reshape(1).astype(F32)

        w = w_in[l]
        wqk = jnp.concatenate([w[:, 0:1536], w[:, 4864:5376], w[:, 4608:4736], w[:, 5888:6016]], axis=1).astype(BF16)
        wg = w[:, 1536:4608].astype(BF16)
        wv = jnp.concatenate([w[:, 5376:5888], w[:, 4736:4864], w[:, 6016:6144]], axis=1).astype(BF16)
        q_scale = HEAD_DIM ** -0.5 * LOG2E
        gain = jnp.concatenate([jnp.tile(a_qnorm[l], 8) * q_scale, jnp.tile(b_qnorm[l], 8) * q_scale,
                                jnp.tile(c_qnorm[l], 8) * q_scale, jnp.tile(b_knorm[l], 8),
                                jnp.tile(a_knorm[l], 2), jnp.tile(c_knorm[l], 2)]).reshape(1, QK_COLS)
        subln = b_subln[l].reshape(1, LANES)
        sink = c_sink[l] * LOG2E
        w_router = jnp.concatenate([w_router_group[l], w_router_expert[l],
                                    jnp.zeros((d, LANES - N_GROUPS - N_EXPERTS), F32)], axis=1)
        w_router_hi = w_router.astype(BF16)
        w_router_lo = (w_router - w_router_hi.astype(F32)).astype(BF16)
        merge_w = (w_branch_a[l].astype(BF16), w_branch_b[l].astype(BF16), w_branch_c[l].astype(BF16),
                   w_out[l].astype(BF16), jnp.concatenate([w_router_hi, w_router_lo], axis=1))

        qk, gates, v = _inproj(x, mod(0), mod(1), wqk, wg, wv, gain, ones, cos, sin, use_rope=True, tm=512)
        cqk, cgates, cv = _inproj(ctx, cmod(0), cmod(1), wqk, wg, wv, gain, ones, no_rope, no_rope,
                                  use_rope=False, tm=c_len)
        ya = _gqa(zero_sink, qk, qk, v, cqk, cv, q_blk=0, k_blk=16, v_blk=4, tq=tq_a, tk=tk, n_lat=n_lat,
                  use_sink=False, name="attn_gqa")
        yb = _diff(lam, qk, qk, v, cqk, cv, subln, tq=tq_b, tk=tk, n_lat=n_lat, out_scale=1.0 - lam_init,
                   name="attn_diff")
        yc = _window(sink, qk, qk, v, cqk, cv, tq=128, nb=4)
        x, h2, route, counts = _merge(ya, yb, yc, gates, x, mod(2), mod(3), mod(4), *merge_w,
                                      jnp.zeros((1, LANES), F32), tm=512, name="merge")
        tokens = [h2.reshape(bsz * length, d)]
        routes = [route.reshape(bsz * length, LANES)]
        if need_ctx:
            cya = _gqa(zero_sink, cqk, cqk, cv, cqk, cv, q_blk=0, k_blk=16, v_blk=4, tq=128, tk=tk, n_lat=0,
                       use_sink=False, name="ctx_gqa")
            cyb = _diff(lam, cqk, cqk, cv, cqk, cv, subln, tq=c_len, tk=tk, n_lat=0, out_scale=1.0 - lam_init,
                        name="ctx_diff")
            cyc = _gqa(sink, cqk, cqk, cv, cqk, cv, q_blk=2, k_blk=17, v_blk=5, tq=128, tk=tk, n_lat=0,
                       use_sink=True, name="ctx_sink")
            ctx, hc2, croute, counts = _merge(cya, cyb, cyc, cgates, ctx, cmod(2), cmod(3), cmod(4), *merge_w, counts,
                                              tm=c_len, name="ctx_merge")
            tokens.append(hc2.reshape(bsz * c_len, d))
            routes.append(croute.reshape(bsz * c_len, LANES))

        h_all = tokens[0] if len(tokens) == 1 else jnp.concatenate(tokens, axis=0)
        r_all = routes[0] if len(routes) == 1 else jnp.concatenate(routes, axis=0)
        gidx, tgt, tile_expert, out_rows = _expert_slots(r_all, counts[0, :N_EXPERTS].astype(jnp.int32), tm_moe)
        y = _experts(tile_expert, gidx, tgt, h_all, w_exp_gate, w_exp_up, w_exp_down, layer=l, tm=tm_moe,
                     out_rows=out_rows).reshape(out_rows // 2, 2 * d)
        x = _combine(y, x, mod(5), routes[0], first_token=0, tc=256, name="moe_combine")
        if need_ctx:
            ctx = _combine(y, ctx, cmod(5), routes[1], first_token=bsz * length, tc=256, name="ctx_combine")
    return x
```

```python
import functools
import math

import jax
import jax.numpy as jnp
from jax import lax
from jax.experimental import pallas as pl
from jax.experimental.pallas import tpu as pltpu

F32 = jnp.float32
BF16 = jnp.bfloat16

D_MODEL = 1024
HEAD_DIM = 64
GRID_W = 64
ROPE_THETA = 10000.0
WINDOW = 128
N_GROUPS = 4
EXPERTS_PER_GROUP = 8
N_EXPERTS = N_GROUPS * EXPERTS_PER_GROUP
EXPERT_HIDDEN = D_MODEL // 2
N_MOD = 6
EPS = 1e-6
NEG = -1e30
LOG2E = math.log2(math.e)
LANES = 128
MOD_ROWS = 16

QK_COLS = 2304
G_COLS = 3 * D_MODEL
V_COLS = 768
NORM_CHUNK = 256
SUB = 256

VMEM_LIMIT = 48 * 1024 * 1024


def _params(n_axes, vmem=VMEM_LIMIT, row_dma=False):
    return pltpu.CompilerParams(dimension_semantics=("arbitrary",) * n_axes, vmem_limit_bytes=vmem,
                                disable_bounds_checks=row_dma)


def _resident(shape):
    return pl.BlockSpec(shape, lambda *_: (0,) * len(shape), pipeline_mode=pl.Buffered(1))


def _sigmoid(v):
    return 1.0 / (1.0 + jnp.exp(-v))


def _rms_modulate(v, shift, scale):
    v = v * lax.rsqrt(jnp.mean(v * v, axis=-1, keepdims=True) + EPS)
    return v * (1.0 + scale) + shift


def _ada_kernel(c_ref, w_ref, b_ref, o_ref):
    c = c_ref[...]
    o_ref[0] = jnp.dot(c * _sigmoid(c), w_ref[0], preferred_element_type=F32,
                       precision=lax.Precision.HIGHEST) + b_ref[0]


def _ada(c_rows, w_ada, b_ada):
    depth, d, n = w_ada.shape
    tn = 1536
    return pl.pallas_call(
        _ada_kernel,
        grid=(depth, n // tn),
        in_specs=[pl.BlockSpec((MOD_ROWS, d), lambda l, j: (0, 0)),
                  pl.BlockSpec((1, d, tn), lambda l, j: (l, 0, j)),
                  pl.BlockSpec((1, 1, tn), lambda l, j: (l, 0, j))],
        out_specs=pl.BlockSpec((1, MOD_ROWS, tn), lambda l, j: (l, 0, j)),
        out_shape=jax.ShapeDtypeStruct((depth, MOD_ROWS, n), F32),
        compiler_params=_params(2),
        name="ada",
    )(c_rows, w_ada, b_ada)


def _rope_partner(z):
    lane = lax.broadcasted_iota(jnp.int32, z.shape, 1)
    return jnp.where((lane & 32) == 0, pltpu.roll(z, 96, 1), pltpu.roll(z, 32, 1))


def _inproj_kernel(x_ref, shift_ref, scale_ref, wqk_ref, wg_ref, wv_ref, gain_ref, ones_ref, cos_ref, sin_ref,
                   oqk_ref, og_ref, ov_ref, *, use_rope):
    hb = _rms_modulate(x_ref[0], shift_ref[0], scale_ref[0]).astype(BF16)
    chunks = [slice(c * NORM_CHUNK, (c + 1) * NORM_CHUNK) for c in range(QK_COLS // NORM_CHUNK)]
    ys = [jnp.dot(hb, wqk_ref[:, cols], preferred_element_type=F32) for cols in chunks]
    sss = [jnp.dot((y * y).astype(BF16), ones_ref[...], preferred_element_type=F32) for y in ys]
    for c in range(G_COLS // 512):
        cols = slice(c * 512, (c + 1) * 512)
        og_ref[0, :, cols] = _sigmoid(jnp.dot(hb, wg_ref[:, cols], preferred_element_type=F32)).astype(BF16)
    ov_ref[0] = jnp.dot(hb, wv_ref[...], preferred_element_type=F32).astype(BF16)
    for cols, y, ss in zip(chunks, ys, sss):
        yn = y * lax.rsqrt(ss * (1.0 / HEAD_DIM) + EPS) * gain_ref[:, cols]
        if use_rope:
            halves = []
            for k in range(NORM_CHUNK // LANES):
                z = yn[:, k * LANES:(k + 1) * LANES]
                halves.append(z * cos_ref[...] + _rope_partner(z) * sin_ref[...])
            yn = jnp.concatenate(halves, axis=1)
        oqk_ref[0, :, cols] = yn.astype(BF16)


def _inproj(x, shift, scale, wqk, wg, wv, gain, ones, cos, sin, *, use_rope, tm):
    bx, t, d = x.shape
    row = lambda b, i: (b, 0, 0)
    tile = lambda b, i: (b, i, 0)
    return pl.pallas_call(
        functools.partial(_inproj_kernel, use_rope=use_rope),
        grid=(bx, t // tm),
        in_specs=[pl.BlockSpec((1, tm, d), tile),
                  pl.BlockSpec((1, 1, d), row), pl.BlockSpec((1, 1, d), row),
                  _resident((d, QK_COLS)), _resident((d, G_COLS)), _resident((d, V_COLS)),
                  _resident((1, QK_COLS)), _resident((NORM_CHUNK, NORM_CHUNK)),
                  pl.BlockSpec((tm, LANES), lambda b, i: (i, 0)), pl.BlockSpec((tm, LANES), lambda b, i: (i, 0))],
        out_specs=[pl.BlockSpec((1, tm, QK_COLS), tile), pl.BlockSpec((1, tm, G_COLS), tile),
                   pl.BlockSpec((1, tm, V_COLS), tile)],
        out_shape=[jax.ShapeDtypeStruct((bx, t, QK_COLS), BF16), jax.ShapeDtypeStruct((bx, t, G_COLS), BF16),
                   jax.ShapeDtypeStruct((bx, t, V_COLS), BF16)],
        compiler_params=_params(2),
        name="inproj_rope" if use_rope else "inproj_ctx",
    )(x, shift, scale, wqk, wg, wv, gain, ones, cos, sin)


def _qk(qs, k):
    return lax.dot_general(qs, k, (((1,), (1,)), ((), ())), preferred_element_type=F32)


def _online_softmax(streams, state, kl_ref, vl_ref, kc_ref, vc_ref, n_lat, tk):
    n = len(streams)
    state = [state[4 * i:4 * i + 4] for i in range(n)]
    for (qs, _, m0, l0), (qs_ref, m_ref, l_ref, acc_ref) in zip(streams, state):
        lane = lax.broadcasted_iota(jnp.int32, m_ref.shape, 1)
        qs_ref[...] = qs
        m_ref[...] = jnp.broadcast_to(m0, m_ref.shape)
        l_ref[...] = jnp.where(lane == 0, l0, 0.0)
        acc_ref[...] = jnp.zeros(acc_ref.shape, F32)

    def scores(i, key_block, width):
        qs = state[i][0][...]
        return [_qk(qs, key_block(c)) for c in range(width // SUB)]

    def update(i, subs, value_block):
        _, m_ref, l_ref, acc_ref = state[i]
        m = m_ref[...]
        top = subs[0]
        for s in subs[1:]:
            top = jnp.maximum(top, s)
        top = functools.reduce(jnp.maximum, [top[:, k * LANES:(k + 1) * LANES] for k in range(SUB // LANES)])
        m_new = jnp.maximum(m, jnp.max(top, axis=-1, keepdims=True))
        alpha = jnp.exp2(m - m_new)
        m_wide = jnp.concatenate([m_new] * (SUB // LANES), axis=1)
        l = alpha * l_ref[...]
        acc = alpha * acc_ref[...]
        for c, s in enumerate(subs):
            p = jnp.exp2(s - m_wide)
            for k in range(SUB // LANES):
                l = l + p[:, k * LANES:(k + 1) * LANES]
            acc = acc + jnp.dot(p.astype(BF16), value_block(c), preferred_element_type=F32)
        m_ref[...] = m_new
        l_ref[...] = l
        acc_ref[...] = acc

    def step(key_block, value_block, width):
        subs = [scores(i, functools.partial(key_block, lanes), width) for i, (_, lanes, _, _) in enumerate(streams)]
        for i, (_, lanes, _, _) in enumerate(streams):
            update(i, subs[i], functools.partial(value_block, lanes))

    if n_lat:
        def body(j, carry):
            rows = lambda c: pl.ds(pl.multiple_of(j * tk + c * SUB, SUB), SUB)
            step(lambda lanes, c: kl_ref[0, rows(c), lanes], lambda lanes, c: vl_ref[0, rows(c), lanes], tk)
            return carry
        lax.fori_loop(0, n_lat, body, 0)
    step(lambda lanes, c: kc_ref[0, c * SUB:(c + 1) * SUB, lanes],
         lambda lanes, c: vc_ref[0, c * SUB:(c + 1) * SUB, lanes], kc_ref.shape[1])
    return [acc_ref[...] / jnp.sum(l_ref[...], axis=-1, keepdims=True) for _, _, l_ref, acc_ref in state]


def _softmax_state(n_streams, rows):
    return [pltpu.VMEM((rows, LANES), BF16), pltpu.VMEM((rows, LANES), F32), pltpu.VMEM((rows, LANES), F32),
            pltpu.VMEM((rows, LANES), F32)] * n_streams


def _stack_pair(q_ref, c, tq, row0=0):
    g = c // 2
    lane = lax.broadcasted_iota(jnp.int32, (tq, LANES), 1)
    keep = (lane >= HEAD_DIM) if g else (lane < HEAD_DIM)
    z = q_ref[0, row0:row0 + tq, c * LANES:(c + 1) * LANES].astype(F32)
    swapped = pltpu.roll(z, HEAD_DIM, 1)
    first, second = (z, swapped) if g == 0 else (swapped, z)
    return jnp.concatenate([jnp.where(keep, first, 0.0).astype(BF16), jnp.where(keep, second, 0.0).astype(BF16)], axis=0)


def _unstack_pair(o, c, tq):
    lane = lax.broadcasted_iota(jnp.int32, (tq, LANES), 1)
    a, b = o[:tq], o[tq:]
    if c // 2 == 0:
        b = pltpu.roll(b, HEAD_DIM, 1)
    else:
        a = pltpu.roll(a, HEAD_DIM, 1)
    return jnp.where(lane < HEAD_DIM, a, b)


def _sink_rows(sink_ref, heads, tq):
    return jnp.concatenate([jnp.full((tq, 1), sink_ref[h], F32) for h in heads], axis=0)


def _gqa_kernel(sink_ref, q_ref, kl_ref, vl_ref, kc_ref, vc_ref, o_ref, *state, tq, tk, n_lat, use_sink):
    streams = []
    for c in range(4):
        if use_sink:
            m0, l0 = _sink_rows(sink_ref, (2 * c, 2 * c + 1), tq), jnp.ones((2 * tq, 1), F32)
        else:
            m0, l0 = jnp.full((2 * tq, 1), NEG, F32), jnp.zeros((2 * tq, 1), F32)
        streams.append((_stack_pair(q_ref, c, tq), slice(0, LANES), m0, l0))
    outs = _online_softmax(streams, state, kl_ref, vl_ref, kc_ref, vc_ref, n_lat, tk)
    for c, o in enumerate(outs):
        o_ref[0, :, c * LANES:(c + 1) * LANES] = _unstack_pair(o, c, tq).astype(BF16)


def _diff_kernel(lam_ref, q_ref, kl_ref, vl_ref, kc_ref, vc_ref, gain_ref, o_ref, *state, tq, tk, n_lat, out_scale):
    lane = lax.broadcasted_iota(jnp.int32, (tq, LANES), 1)
    streams = []
    for n in range(4):
        lanes = slice(n * LANES, (n + 1) * LANES)
        z = q_ref[0, :, lanes]
        zero = jnp.zeros_like(z)
        qs = jnp.concatenate([jnp.where(lane < HEAD_DIM, z, zero), jnp.where(lane >= HEAD_DIM, z, zero)], axis=0)
        streams.append((qs, lanes, jnp.full((2 * tq, 1), NEG, F32), jnp.zeros((2 * tq, 1), F32)))
    outs = _online_softmax(streams, state, kl_ref, vl_ref, kc_ref, vc_ref, n_lat, tk)
    for n, o in enumerate(outs):
        d = o[:tq] - lam_ref[0] * o[tq:]
        d = d * lax.rsqrt(jnp.mean(d * d, axis=-1, keepdims=True) + EPS) * gain_ref[...] * out_scale
        o_ref[0, :, n * LANES:(n + 1) * LANES] = d.astype(BF16)


def _lane_blocks(a):
    return [a[:, k * LANES:(k + 1) * LANES] for k in range(a.shape[1] // LANES)]


def _window_kernel(sink_ref, q_ref, kl_ref, vl_ref, kc_ref, vc_ref, o_ref, *, tq, nb, ppc, length):
    span = tq + 2 * WINDOW
    kc = kc_ref[0]
    vc = vc_ref[0]
    rel = lax.broadcasted_iota(jnp.int32, (tq, span), 1) - lax.broadcasted_iota(jnp.int32, (tq, span), 0)
    chains = [(j, tuple(range(c, c + ppc))) for j in range(nb) for c in range(0, 4, ppc)]
    heads = 2 * ppc
    starts = []
    for j in range(nb):
        first = (pl.program_id(1) * nb + j) * tq
        starts.append((first, pl.multiple_of(jnp.clip(first - WINDOW, 0, length - span), LANES)))
    sw, sc = [], []
    for j, cs in chains:
        first, start = starts[j]
        valid = jnp.abs(rel + (start - first)) <= WINDOW
        qs = jnp.concatenate([_stack_pair(q_ref, c, tq, j * tq) for c in cs], axis=0)
        sw.append(jnp.where(valid[None], _qk(qs, kl_ref[0, pl.ds(start, span), :]).reshape(heads, tq, span),
                            NEG).reshape(heads * tq, span))
        sc.append(_qk(qs, kc))
    snk = [_sink_rows(sink_ref, range(2 * cs[0], 2 * cs[-1] + 2), tq) for _, cs in chains]
    top = [jnp.max(functools.reduce(jnp.maximum, _lane_blocks(a) + _lane_blocks(b)), axis=-1, keepdims=True)
           for a, b in zip(sw, sc)]
    m = [jnp.maximum(t, s) for t, s in zip(top, snk)]
    pw = [jnp.exp2(a - mm) for a, mm in zip(sw, m)]
    pc = [jnp.exp2(b - mm) for b, mm in zip(sc, m)]
    tot = [jnp.sum(functools.reduce(jnp.add, _lane_blocks(a) + _lane_blocks(b)), axis=-1, keepdims=True)
           for a, b in zip(pw, pc)]
    l = [t + jnp.exp2(s - mm) for t, s, mm in zip(tot, snk, m)]
    o = [jnp.dot(a.astype(BF16), vl_ref[0, pl.ds(starts[j][1], span), :], preferred_element_type=F32)
         + jnp.dot(b.astype(BF16), vc, preferred_element_type=F32) for (j, _), a, b in zip(chains, pw, pc)]
    o = [a / ll for a, ll in zip(o, l)]
    for (j, cs), a in zip(chains, o):
        for i, c in enumerate(cs):
            o_ref[0, j * tq:(j + 1) * tq, c * LANES:(c + 1) * LANES] = _unstack_pair(
                a[2 * i * tq:(2 * i + 2) * tq], c, tq).astype(BF16)


_SMEM = pl.BlockSpec(memory_space=pltpu.SMEM)


def _kv_specs(klat, kctx, k_blk, v_blk, width):
    s_lat, s_ctx = klat.shape[1], kctx.shape[1]
    return [pl.BlockSpec((1, s_lat, width), lambda b, i: (b, 0, k_blk)),
            pl.BlockSpec((1, s_lat, width), lambda b, i: (b, 0, v_blk)),
            pl.BlockSpec((1, s_ctx, width), lambda b, i: (b, 0, k_blk)),
            pl.BlockSpec((1, s_ctx, width), lambda b, i: (b, 0, v_blk))]


def _gqa(sink, q, klat, vlat, kctx, vctx, *, q_blk, k_blk, v_blk, tq, tk, n_lat, use_sink, name):
    bx, t, _ = q.shape
    return pl.pallas_call(
        functools.partial(_gqa_kernel, tq=tq, tk=tk, n_lat=n_lat, use_sink=use_sink),
        grid=(bx, t // tq),
        in_specs=[_SMEM, pl.BlockSpec((1, tq, 512), lambda b, i: (b, i, q_blk))]
        + _kv_specs(klat, kctx, k_blk, v_blk, LANES),
        out_specs=pl.BlockSpec((1, tq, 512), lambda b, i: (b, i, 0)),
        out_shape=jax.ShapeDtypeStruct((bx, t, 512), BF16),
        scratch_shapes=_softmax_state(4, 2 * tq),
        compiler_params=_params(2),
        name=name,
    )(sink, q, klat, vlat, kctx, vctx)


def _diff(lam, q, klat, vlat, kctx, vctx, gain, *, tq, tk, n_lat, out_scale, name):
    bx, t, _ = q.shape
    return pl.pallas_call(
        functools.partial(_diff_kernel, tq=tq, tk=tk, n_lat=n_lat, out_scale=out_scale),
        grid=(bx, t // tq),
        in_specs=[_SMEM, pl.BlockSpec((1, tq, 512), lambda b, i: (b, i, 1))]
        + _kv_specs(klat, kctx, 3, 0, 512) + [pl.BlockSpec((1, LANES), lambda b, i: (0, 0))],
        out_specs=pl.BlockSpec((1, tq, 512), lambda b, i: (b, i, 0)),
        out_shape=jax.ShapeDtypeStruct((bx, t, 512), BF16),
        scratch_shapes=_softmax_state(4, 2 * tq),
        compiler_params=_params(2),
        name=name,
    )(lam, q, klat, vlat, kctx, vctx, gain)


def _window(sink, q, klat, vlat, kctx, vctx, *, tq, nb, ppc):
    bx, t, _ = q.shape
    return pl.pallas_call(
        functools.partial(_window_kernel, tq=tq, nb=nb, ppc=ppc, length=t),
        grid=(bx, t // (nb * tq)),
        in_specs=[_SMEM, pl.BlockSpec((1, nb * tq, 512), lambda b, i: (b, i, 2))]
        + _kv_specs(klat, kctx, 17, 5, LANES),
        out_specs=pl.BlockSpec((1, nb * tq, 512), lambda b, i: (b, i, 0)),
        out_shape=jax.ShapeDtypeStruct((bx, t, 512), BF16),
        compiler_params=_params(2),
        name="attn_window",
    )(sink, q, klat, vlat, kctx, vctx)


def _route(logits, tri, base):
    lane = lax.broadcasted_iota(jnp.int32, logits.shape, 1).astype(F32)
    first = lambda hit: jnp.min(jnp.where(hit, lane, float(LANES)), axis=-1, keepdims=True)
    gl = jnp.where(lane < N_GROUPS, logits, NEG)
    gmax = jnp.max(gl, axis=-1, keepdims=True)
    gidx = first(gl == gmax)
    gw = 1.0 / jnp.sum(jnp.exp(gl - gmax), axis=-1, keepdims=True)
    lo = N_GROUPS + gidx * EXPERTS_PER_GROUP
    el = jnp.where((lane >= lo) & (lane < lo + EXPERTS_PER_GROUP), logits, NEG)
    v1 = jnp.max(el, axis=-1, keepdims=True)
    i1 = first(el == v1)
    el = jnp.where(lane == i1, NEG, el)
    v2 = jnp.max(el, axis=-1, keepdims=True)
    i2 = first(el == v2)
    e = jnp.exp(v2 - v1)
    w1 = gw / (1.0 + e)
    w2 = gw * e / (1.0 + e)
    e1 = i1 - N_GROUPS
    e2 = i2 - N_GROUPS
    hit1 = jnp.where(lane == e1, 1.0, 0.0)
    hit2 = jnp.where(lane == e2, 1.0, 0.0)
    before1 = jnp.dot(tri, hit1.astype(BF16), preferred_element_type=F32) + base
    base = base + jnp.sum(hit1, axis=0, keepdims=True)
    before2 = jnp.dot(tri, hit2.astype(BF16), preferred_element_type=F32) + base
    base = base + jnp.sum(hit2, axis=0, keepdims=True)
    r1 = jnp.sum(hit1 * before1, axis=-1, keepdims=True)
    r2 = jnp.sum(hit2 * before2, axis=-1, keepdims=True)
    out = jnp.zeros_like(logits)
    for k, val in enumerate((e1, e2, w1, w2, r1, r2)):
        out = jnp.where(lane == k, val, out)
    return out, base


def _merge_kernel(ya_ref, yb_ref, yc_ref, g_ref, x_ref, gate_ref, shift_ref, scale_ref,
                  wa_ref, wb_ref, wc_ref, wo_ref, wr_ref, tri_ref, base_ref, xo_ref, h_ref, r_ref, cnt_ref):
    d = D_MODEL

    @pl.when((pl.program_id(0) == 0) & (pl.program_id(1) == 0))
    def _():
        cnt_ref[...] = base_ref[...]

    m = None
    for k, (y_ref, w_ref) in enumerate(((ya_ref, wa_ref), (yb_ref, wb_ref), (yc_ref, wc_ref))):
        t = g_ref[0, :, k * d:(k + 1) * d].astype(F32) * jnp.dot(y_ref[0], w_ref[...], preferred_element_type=F32)
        m = t if m is None else m + t
    xn = x_ref[0] + gate_ref[0] * jnp.dot(m.astype(BF16), wo_ref[...], preferred_element_type=F32)
    xo_ref[0] = xn
    h2 = _rms_modulate(xn, shift_ref[0], scale_ref[0])
    h_ref[0] = h2
    hi = h2.astype(BF16)
    lo = (h2 - hi.astype(F32)).astype(BF16)
    both = jnp.dot(hi, wr_ref[...], preferred_element_type=F32)
    logits = (both[:, :LANES] + both[:, LANES:]) + jnp.dot(lo, wr_ref[:, :LANES], preferred_element_type=F32)
    r_ref[0], cnt_ref[...] = _route(logits, tri_ref[...], cnt_ref[...])


def _merge(ya, yb, yc, g, x, gate, shift, scale, wa, wb, wc, wo, wr, base, *, tm, name):
    bx, t, d = x.shape
    row = lambda b, i: (b, 0, 0)
    tile = lambda b, i: (b, i, 0)
    ids = jnp.arange(tm, dtype=jnp.int32)
    tri = (ids[None, :] < ids[:, None]).astype(BF16)
    return pl.pallas_call(
        _merge_kernel,
        grid=(bx, t // tm),
        in_specs=[pl.BlockSpec((1, tm, 512), tile)] * 3
        + [pl.BlockSpec((1, tm, G_COLS), tile), pl.BlockSpec((1, tm, d), tile)]
        + [pl.BlockSpec((1, 1, d), row)] * 3
        + [_resident((512, d))] * 3 + [_resident((d, d)), _resident((d, 2 * LANES)), _resident((tm, tm)),
                                      _resident((1, LANES))],
        out_specs=[pl.BlockSpec((1, tm, d), tile), pl.BlockSpec((1, tm, d), tile), pl.BlockSpec((1, tm, LANES), tile),
                   pl.BlockSpec((1, LANES), lambda b, i: (0, 0))],
        out_shape=[jax.ShapeDtypeStruct((bx, t, d), F32), jax.ShapeDtypeStruct((bx, t, d), F32),
                   jax.ShapeDtypeStruct((bx, t, LANES), F32), jax.ShapeDtypeStruct((1, LANES), F32)],
        compiler_params=_params(2),
        name=name,
    )(ya, yb, yc, g, x, gate, shift, scale, wa, wb, wc, wo, wr, tri, base)


def _dispatch_kernel(dest_ref, h_ref, xs_in_ref, xs_ref, sem, *, tokens):
    del xs_in_ref

    def row_copy(r, d):
        return pltpu.make_async_copy(h_ref.at[pl.ds(r, 1), :], xs_ref.at[pl.ds(d, 1), :], sem)

    def issue(r, carry):
        for k in range(2):
            row_copy(r, dest_ref[0, 0, 2 * r + k]).start(priority=k)
        return carry

    lax.fori_loop(0, tokens, issue, 0, unroll=8)
    for _ in range(2):
        pltpu.make_async_copy(h_ref, xs_ref.at[pl.ds(0, tokens), :], sem).wait()


def _dispatch(dest, h, xs, *, td):
    n, d = h.shape
    return pl.pallas_call(
        functools.partial(_dispatch_kernel, tokens=td),
        grid=(n // td,),
        in_specs=[pl.BlockSpec((1, 1, 2 * td), lambda i: (i, 0, 0), memory_space=pltpu.SMEM),
                  pl.BlockSpec((td, d), lambda i: (i, 0)),
                  pl.BlockSpec(memory_space=pl.ANY)],
        out_specs=pl.BlockSpec(memory_space=pl.ANY),
        out_shape=jax.ShapeDtypeStruct(xs.shape, xs.dtype),
        scratch_shapes=[pltpu.SemaphoreType.DMA(())],
        input_output_aliases={2: 0},
        compiler_params=_params(1, row_dma=True),
        name="moe_dispatch",
    )(dest.reshape(n // td, 1, 2 * td), h, xs)


def _experts_kernel(te_ref, used_ref, xs_ref, wg_ref, wu_ref, wd_ref, y_ref, wg_bf, wu_bf, wd_bf):
    t = pl.program_id(0)
    live = t < used_ref[0]

    @pl.when(live & ((t == 0) | (te_ref[t] != te_ref[jnp.maximum(t - 1, 0)])))
    def _():
        wg_bf[...] = wg_ref[0, 0].astype(BF16)
        wu_bf[...] = wu_ref[0, 0].astype(BF16)
        wd_bf[...] = wd_ref[0, 0].astype(BF16)

    @pl.when(live)
    def _():
        xb = xs_ref[...].astype(BF16)
        a = jnp.dot(xb, wg_bf[...], preferred_element_type=F32)
        u = jnp.dot(xb, wu_bf[...], preferred_element_type=F32)
        y_ref[...] = jnp.dot((a * _sigmoid(a) * u).astype(BF16), wd_bf[...], preferred_element_type=F32)

    @pl.when(t >= used_ref[0])
    def _():
        y_ref[...] = jnp.zeros_like(y_ref)


def _experts(tile_expert, n_used, xs, wg, wu, wd, *, layer, tm):
    p, d = xs.shape
    hid = wg.shape[3]
    weights = lambda t, te, nu: (layer, te[t], 0, 0)
    return pl.pallas_call(
        _experts_kernel,
        grid_spec=pltpu.PrefetchScalarGridSpec(
            num_scalar_prefetch=2,
            grid=(p // tm,),
            in_specs=[pl.BlockSpec((tm, d), lambda t, te, nu: (t, 0)),
                      pl.BlockSpec((1, 1, d, hid), weights), pl.BlockSpec((1, 1, d, hid), weights),
                      pl.BlockSpec((1, 1, hid, d), weights)],
            out_specs=pl.BlockSpec((tm, d), lambda t, te, nu: (t, 0)),
            scratch_shapes=[pltpu.VMEM((d, hid), BF16), pltpu.VMEM((d, hid), BF16), pltpu.VMEM((hid, d), BF16)]),
        out_shape=jax.ShapeDtypeStruct((p, d), F32),
        compiler_params=_params(1),
        name="moe_experts",
    )(tile_expert, n_used, xs, wg, wu, wd)


def _combine_kernel(dest_ref, y_ref, x_ref, gate_ref, r_ref, o_ref, buf, sem, *, tokens):
    def row_copy(r, k, d):
        return pltpu.make_async_copy(y_ref.at[pl.ds(d, 1), :], buf.at[k, pl.ds(r, 1), :], sem)

    def issue(r, carry):
        for k in range(2):
            row_copy(r, k, dest_ref[0, 0, 2 * r + k]).start(priority=k)
        return carry

    lax.fori_loop(0, tokens, issue, 0, unroll=8)
    for k in range(2):
        pltpu.make_async_copy(y_ref.at[pl.ds(0, tokens), :], buf.at[k], sem).wait()
    w = r_ref[...]
    o_ref[0] = x_ref[0] + gate_ref[0] * (w[:, 2:3] * buf[0] + w[:, 3:4] * buf[1])


def _combine(dest, y, x, gate, route, *, tc, name):
    bx, t, d = x.shape
    per = t // tc
    return pl.pallas_call(
        functools.partial(_combine_kernel, tokens=tc),
        grid=(bx * per,),
        in_specs=[pl.BlockSpec((1, 1, 2 * tc), lambda i: (i, 0, 0), memory_space=pltpu.SMEM),
                  pl.BlockSpec(memory_space=pl.ANY),
                  pl.BlockSpec((1, tc, d), lambda i: (i // per, i % per, 0)),
                  pl.BlockSpec((1, 1, d), lambda i: (i // per, 0, 0)),
                  pl.BlockSpec((tc, LANES), lambda i: (i, 0))],
        out_specs=pl.BlockSpec((1, tc, d), lambda i: (i // per, i % per, 0)),
        out_shape=jax.ShapeDtypeStruct(x.shape, F32),
        scratch_shapes=[pltpu.VMEM((2, tc, d), F32), pltpu.SemaphoreType.DMA(())],
        compiler_params=_params(1, row_dma=True),
        name=name,
    )(dest.reshape(bx * per, 1, 2 * tc), y, x, gate, route)


def _expert_slots(routes, counts, tm):
    n_assign = 2 * sum(r.shape[0] for r in routes)
    padded = ((counts + tm - 1) // tm) * tm
    ends = jnp.cumsum(padded)
    starts = ends - padded
    experts = jnp.arange(N_EXPERTS, dtype=jnp.int32)
    dests = []
    for r in routes:
        e = r[:, 0:2].astype(jnp.int32)
        rank = r[:, 4:6].astype(jnp.int32)
        start = jnp.sum(jnp.where(e[:, :, None] == experts, starts, 0), axis=-1)
        dests.append((start + rank).reshape(-1))
    n_tiles = n_assign // tm + N_EXPERTS
    tiles = jnp.arange(n_tiles, dtype=jnp.int32)
    tile_expert = jnp.minimum(jnp.sum((tiles[:, None] >= (ends // tm)[None, :]).astype(jnp.int32), axis=1),
                              N_EXPERTS - 1)
    return dests, tile_expert, (ends[-1:] // tm).astype(jnp.int32), n_tiles


def _rope_tables(length):
    pairs = HEAD_DIM // 4
    pos = jnp.arange(length, dtype=jnp.int32)
    row = (pos // GRID_W).astype(F32)
    col = (pos % GRID_W).astype(F32)
    freqs = ROPE_THETA ** (-jnp.arange(pairs, dtype=F32) / pairs)
    ang = jnp.concatenate([row[:, None] * freqs, col[:, None] * freqs], axis=-1)
    cos = jnp.tile(jnp.cos(ang), (1, 4))
    sin = jnp.tile(jnp.concatenate([-jnp.sin(ang), jnp.sin(ang)], axis=-1), (1, 2))
    return cos, sin


def kernel(x, c, ctx, c_ctx, w_ada, b_ada, w_in, a_qnorm, a_knorm, b_qnorm, b_knorm, c_qnorm, c_knorm, lambda_q1, lambda_k1, lambda_q2, lambda_k2, b_subln, c_sink, w_branch_a, w_branch_b, w_branch_c, w_out, w_router_group, w_router_expert, w_exp_gate, w_exp_up, w_exp_down):
    bsz, length, d = x.shape
    c_len = ctx.shape[1]
    depth = w_ada.shape[0]
    assert d == D_MODEL and bsz + 1 <= MOD_ROWS and length % 512 == 0 and c_len == 256

    cos, sin = _rope_tables(length)
    no_rope = jnp.zeros((c_len, LANES), F32)
    head_ids = jnp.arange(NORM_CHUNK, dtype=jnp.int32) // HEAD_DIM
    ones = (head_ids[:, None] == head_ids[None, :]).astype(BF16)
    c_rows = jnp.concatenate([c, c_ctx[None, :], jnp.zeros((MOD_ROWS - bsz - 1, d), F32)], axis=0)
    mod_all = _ada(c_rows, w_ada, b_ada.reshape(depth, 1, -1))
    zero_sink = jnp.zeros((8,), F32)

    tq_a, tq_b = 256, 256
    tk = 2048 if length % 2048 == 0 else 512
    n_lat = length // tk
    tm_moe = 256

    for l in range(depth):
        need_ctx = l < depth - 1
        mod = lambda k: mod_all[l, :bsz, k * d:(k + 1) * d].reshape(bsz, 1, d)
        cmod = lambda k: jnp.broadcast_to(mod_all[l, bsz, k * d:(k + 1) * d], (bsz, 1, d))
        lam_init = 0.8 - 0.6 * math.exp(-0.3 * l)
        lam = (jnp.exp(jnp.sum(lambda_q1[l] * lambda_k1[l])) - jnp.exp(jnp.sum(lambda_q2[l] * lambda_k2[l]))
               + lam_init).reshape(1).astype(F32)

        w = w_in[l]
        wqk = jnp.concatenate([w[:, 0:1536], w[:, 4864:5376], w[:, 4608:4736], w[:, 5888:6016]], axis=1).astype(BF16)
        wg = w[:, 1536:4608].astype(BF16)
        wv = jnp.concatenate([w[:, 5376:5888], w[:, 4736:4864], w[:, 6016:6144]], axis=1).astype(BF16)
        q_scale = HEAD_DIM ** -0.5 * LOG2E
        gain = jnp.concatenate([jnp.tile(a_qnorm[l], 8) * q_scale, jnp.tile(b_qnorm[l], 8) * q_scale,
                                jnp.tile(c_qnorm[l], 8) * q_scale, jnp.tile(b_knorm[l], 8),
                                jnp.tile(a_knorm[l], 2), jnp.tile(c_knorm[l], 2)]).reshape(1, QK_COLS)
        subln = b_subln[l].reshape(1, LANES)
        sink = c_sink[l] * LOG2E
        w_router = jnp.concatenate([w_router_group[l], w_router_expert[l],
                                    jnp.zeros((d, LANES - N_GROUPS - N_EXPERTS), F32)], axis=1)
        w_router_hi = w_router.astype(BF16)
        w_router_lo = (w_router - w_router_hi.astype(F32)).astype(BF16)
        merge_w = (w_branch_a[l].astype(BF16), w_branch_b[l].astype(BF16), w_branch_c[l].astype(BF16),
                   w_out[l].astype(BF16), jnp.concatenate([w_router_hi, w_router_lo], axis=1))

        qk, gates, v = _inproj(x, mod(0), mod(1), wqk, wg, wv, gain, ones, cos, sin, use_rope=True, tm=512)
        cqk, cgates, cv = _inproj(ctx, cmod(0), cmod(1), wqk, wg, wv, gain, ones, no_rope, no_rope,
                                  use_rope=False, tm=c_len)
        ya = _gqa(zero_sink, qk, qk, v, cqk, cv, q_blk=0, k_blk=16, v_blk=4, tq=tq_a, tk=tk, n_lat=n_lat,
                  use_sink=False, name="attn_gqa")
        yb = _diff(lam, qk, qk, v, cqk, cv, subln, tq=tq_b, tk=tk, n_lat=n_lat, out_scale=1.0 - lam_init,
                   name="attn_diff")
        yc = _window(sink, qk, qk, v, cqk, cv, tq=128, nb=4, ppc=1)
        x, h2, route, counts = _merge(ya, yb, yc, gates, x, mod(2), mod(3), mod(4), *merge_w,
                                      jnp.zeros((1, LANES), F32), tm=512, name="merge")
        tokens = [h2.reshape(bsz * length, d)]
        routes = [route.reshape(bsz * length, LANES)]
        if need_ctx:
            cya = _gqa(zero_sink, cqk, cqk, cv, cqk, cv, q_blk=0, k_blk=16, v_blk=4, tq=128, tk=tk, n_lat=0,
                       use_sink=False, name="ctx_gqa")
            cyb = _diff(lam, cqk, cqk, cv, cqk, cv, subln, tq=c_len, tk=tk, n_lat=0, out_scale=1.0 - lam_init,
                        name="ctx_diff")
            cyc = _gqa(sink, cqk, cqk, cv, cqk, cv, q_blk=2, k_blk=17, v_blk=5, tq=128, tk=tk, n_lat=0,
                       use_sink=True, name="ctx_sink")
            ctx, hc2, croute, counts = _merge(cya, cyb, cyc, cgates, ctx, cmod(2), cmod(3), cmod(4), *merge_w, counts,
                                              tm=c_len, name="ctx_merge")
            tokens.append(hc2.reshape(bsz * c_len, d))
            routes.append(croute.reshape(bsz * c_len, LANES))

        dests, tile_expert, n_used, n_tiles = _expert_slots(routes, counts[0, :N_EXPERTS].astype(jnp.int32), tm_moe)
        xs = jnp.zeros((n_tiles * tm_moe, d), F32)
        for t, dest in zip(tokens, dests):
            xs = _dispatch(dest, t, xs, td=256)
        y = _experts(tile_expert, n_used, xs, w_exp_gate, w_exp_up, w_exp_down, layer=l, tm=tm_moe)
        x = _combine(dests[0], y, x, mod(5), routes[0], tc=256, name="moe_combine")
        if need_ctx:
            ctx = _combine(dests[1], y, ctx, cmod(5), routes[1], tc=256, name="ctx_combine")
    return x
```

```python
import functools
import math

import jax
import jax.numpy as jnp
from jax import lax
from jax.experimental import pallas as pl
from jax.experimental.pallas import tpu as pltpu

F32 = jnp.float32
BF16 = jnp.bfloat16

D_MODEL = 1024
HEAD_DIM = 64
GRID_W = 64
ROPE_THETA = 10000.0
WINDOW = 128
N_GROUPS = 4
EXPERTS_PER_GROUP = 8
N_EXPERTS = N_GROUPS * EXPERTS_PER_GROUP
EXPERT_HIDDEN = D_MODEL // 2
N_MOD = 6
EPS = 1e-6
NEG = -1e30
LOG2E = math.log2(math.e)
LANES = 128
MOD_ROWS = 16

QK_COLS = 2304
G_COLS = 3 * D_MODEL
V_COLS = 768
NORM_CHUNK = 256
SUB = 256
GQA_STREAM_ROWS = 256

VMEM_LIMIT = 48 * 1024 * 1024


def _params(n_axes, vmem=VMEM_LIMIT, row_dma=False):
    return pltpu.CompilerParams(dimension_semantics=("arbitrary",) * n_axes, vmem_limit_bytes=vmem,
                                disable_bounds_checks=row_dma)


def _resident(shape):
    return pl.BlockSpec(shape, lambda *_: (0,) * len(shape), pipeline_mode=pl.Buffered(1))


def _sigmoid(v):
    return 1.0 / (1.0 + jnp.exp(-v))


def _rms_modulate(v, shift, scale):
    v = v * lax.rsqrt(jnp.mean(v * v, axis=-1, keepdims=True) + EPS)
    return v * (1.0 + scale) + shift


def _ada_kernel(c_ref, w_ref, b_ref, o_ref):
    c = c_ref[...]
    o_ref[0] = jnp.dot(c * _sigmoid(c), w_ref[0], preferred_element_type=F32,
                       precision=lax.Precision.HIGHEST) + b_ref[0]


def _ada(c_rows, w_ada, b_ada):
    depth, d, n = w_ada.shape
    tn = 1536
    return pl.pallas_call(
        _ada_kernel,
        grid=(depth, n // tn),
        in_specs=[pl.BlockSpec((MOD_ROWS, d), lambda l, j: (0, 0)),
                  pl.BlockSpec((1, d, tn), lambda l, j: (l, 0, j)),
                  pl.BlockSpec((1, 1, tn), lambda l, j: (l, 0, j))],
        out_specs=pl.BlockSpec((1, MOD_ROWS, tn), lambda l, j: (l, 0, j)),
        out_shape=jax.ShapeDtypeStruct((depth, MOD_ROWS, n), F32),
        compiler_params=_params(2),
        name="ada",
    )(c_rows, w_ada, b_ada)


def _rope_partner(z):
    lane = lax.broadcasted_iota(jnp.int32, z.shape, 1)
    return jnp.where((lane & 32) == 0, pltpu.roll(z, 96, 1), pltpu.roll(z, 32, 1))


def _inproj_kernel(x_ref, shift_ref, scale_ref, wqk_ref, wg_ref, wv_ref, gain_ref, ones_ref, cos_ref, sin_ref,
                   oqk_ref, og_ref, ov_ref, *, use_rope):
    hb = _rms_modulate(x_ref[0], shift_ref[0], scale_ref[0]).astype(BF16)
    chunks = [slice(c * NORM_CHUNK, (c + 1) * NORM_CHUNK) for c in range(QK_COLS // NORM_CHUNK)]
    ys = [jnp.dot(hb, wqk_ref[:, cols], preferred_element_type=F32) for cols in chunks]
    sss = [jnp.dot((y * y).astype(BF16), ones_ref[...], preferred_element_type=F32) for y in ys]
    for c in range(G_COLS // 512):
        cols = slice(c * 512, (c + 1) * 512)
        og_ref[0, :, cols] = _sigmoid(jnp.dot(hb, wg_ref[:, cols], preferred_element_type=F32)).astype(BF16)
    ov_ref[0] = jnp.dot(hb, wv_ref[...], preferred_element_type=F32).astype(BF16)
    for cols, y, ss in zip(chunks, ys, sss):
        yn = y * lax.rsqrt(ss * (1.0 / HEAD_DIM) + EPS) * gain_ref[:, cols]
        if use_rope:
            halves = []
            for k in range(NORM_CHUNK // LANES):
                z = yn[:, k * LANES:(k + 1) * LANES]
                halves.append(z * cos_ref[...] + _rope_partner(z) * sin_ref[...])
            yn = jnp.concatenate(halves, axis=1)
        oqk_ref[0, :, cols] = yn.astype(BF16)


def _inproj(x, shift, scale, wqk, wg, wv, gain, ones, cos, sin, *, use_rope, tm):
    bx, t, d = x.shape
    row = lambda b, i: (b, 0, 0)
    tile = lambda b, i: (b, i, 0)
    return pl.pallas_call(
        functools.partial(_inproj_kernel, use_rope=use_rope),
        grid=(bx, t // tm),
        in_specs=[pl.BlockSpec((1, tm, d), tile),
                  pl.BlockSpec((1, 1, d), row), pl.BlockSpec((1, 1, d), row),
                  _resident((d, QK_COLS)), _resident((d, G_COLS)), _resident((d, V_COLS)),
                  _resident((1, QK_COLS)), _resident((NORM_CHUNK, NORM_CHUNK)),
                  pl.BlockSpec((tm, LANES), lambda b, i: (i, 0)), pl.BlockSpec((tm, LANES), lambda b, i: (i, 0))],
        out_specs=[pl.BlockSpec((1, tm, QK_COLS), tile), pl.BlockSpec((1, tm, G_COLS), tile),
                   pl.BlockSpec((1, tm, V_COLS), tile)],
        out_shape=[jax.ShapeDtypeStruct((bx, t, QK_COLS), BF16), jax.ShapeDtypeStruct((bx, t, G_COLS), BF16),
                   jax.ShapeDtypeStruct((bx, t, V_COLS), BF16)],
        compiler_params=_params(2),
        name="inproj_rope" if use_rope else "inproj_ctx",
    )(x, shift, scale, wqk, wg, wv, gain, ones, cos, sin)


def _qk(qs, k):
    return lax.dot_general(qs, k, (((1,), (1,)), ((), ())), preferred_element_type=F32)


def _online_softmax(streams, state, kl_ref, vl_ref, kc_ref, vc_ref, n_lat, tk):
    n = len(streams)
    state = [state[4 * i:4 * i + 4] for i in range(n)]
    for (qs, _, m0, l0), (qs_ref, m_ref, l_ref, acc_ref) in zip(streams, state):
        lane = lax.broadcasted_iota(jnp.int32, m_ref.shape, 1)
        qs_ref[...] = qs
        m_ref[...] = jnp.broadcast_to(m0, m_ref.shape)
        l_ref[...] = jnp.where(lane == 0, l0, 0.0)
        acc_ref[...] = jnp.zeros(acc_ref.shape, F32)

    def scores(i, key_block, width):
        qs = state[i][0][...]
        return [_qk(qs, key_block(c)) for c in range(width // SUB)]

    def update(i, subs, value_block):
        _, m_ref, l_ref, acc_ref = state[i]
        m = m_ref[...]
        top = subs[0]
        for s in subs[1:]:
            top = jnp.maximum(top, s)
        top = functools.reduce(jnp.maximum, [top[:, k * LANES:(k + 1) * LANES] for k in range(SUB // LANES)])
        m_new = jnp.maximum(m, jnp.max(top, axis=-1, keepdims=True))
        alpha = jnp.exp2(m - m_new)
        m_wide = jnp.concatenate([m_new] * (SUB // LANES), axis=1)
        l = alpha * l_ref[...]
        acc = alpha * acc_ref[...]
        for c, s in enumerate(subs):
            p = jnp.exp2(s - m_wide)
            for k in range(SUB // LANES):
                l = l + p[:, k * LANES:(k + 1) * LANES]
            acc = acc + jnp.dot(p.astype(BF16), value_block(c), preferred_element_type=F32)
        m_ref[...] = m_new
        l_ref[...] = l
        acc_ref[...] = acc

    def step(key_block, value_block, width):
        subs = [scores(i, functools.partial(key_block, lanes), width) for i, (_, lanes, _, _) in enumerate(streams)]
        for i, (_, lanes, _, _) in enumerate(streams):
            update(i, subs[i], functools.partial(value_block, lanes))

    if n_lat:
        def body(j, carry):
            rows = lambda c: pl.ds(pl.multiple_of(j * tk + c * SUB, SUB), SUB)
            step(lambda lanes, c: kl_ref[0, rows(c), lanes], lambda lanes, c: vl_ref[0, rows(c), lanes], tk)
            return carry
        lax.fori_loop(0, n_lat, body, 0)
    step(lambda lanes, c: kc_ref[0, c * SUB:(c + 1) * SUB, lanes],
         lambda lanes, c: vc_ref[0, c * SUB:(c + 1) * SUB, lanes], kc_ref.shape[1])
    return [acc_ref[...] / jnp.sum(l_ref[...], axis=-1, keepdims=True) for _, _, l_ref, acc_ref in state]


def _softmax_state(n_streams, rows):
    return [pltpu.VMEM((rows, LANES), BF16), pltpu.VMEM((rows, LANES), F32), pltpu.VMEM((rows, LANES), F32),
            pltpu.VMEM((rows, LANES), F32)] * n_streams


def _stack_pair(q_ref, c, tq, row0=0):
    g = c // 2
    lane = lax.broadcasted_iota(jnp.int32, (tq, LANES), 1)
    keep = (lane >= HEAD_DIM) if g else (lane < HEAD_DIM)
    z = q_ref[0, row0:row0 + tq, c * LANES:(c + 1) * LANES].astype(F32)
    swapped = pltpu.roll(z, HEAD_DIM, 1)
    first, second = (z, swapped) if g == 0 else (swapped, z)
    return jnp.concatenate([jnp.where(keep, first, 0.0).astype(BF16), jnp.where(keep, second, 0.0).astype(BF16)], axis=0)


def _unstack_pair(o, c, tq):
    lane = lax.broadcasted_iota(jnp.int32, (tq, LANES), 1)
    a, b = o[:tq], o[tq:]
    if c // 2 == 0:
        b = pltpu.roll(b, HEAD_DIM, 1)
    else:
        a = pltpu.roll(a, HEAD_DIM, 1)
    return jnp.where(lane < HEAD_DIM, a, b)


def _sink_rows(sink_ref, heads, tq):
    return jnp.concatenate([jnp.full((tq, 1), sink_ref[h], F32) for h in heads], axis=0)


def _gqa_kernel(sink_ref, q_ref, kl_ref, vl_ref, kc_ref, vc_ref, o_ref, *state, tq, tk, n_lat, use_sink):
    rs = min(tq, GQA_STREAM_ROWS)
    per_pair = 2 * tq // rs
    streams = []
    for c in range(4):
        pair = _stack_pair(q_ref, c, tq)
        for i in range(per_pair):
            h = 2 * c + i * rs // tq
            if use_sink:
                m0, l0 = jnp.full((rs, 1), sink_ref[h], F32), jnp.ones((rs, 1), F32)
            else:
                m0, l0 = jnp.full((rs, 1), NEG, F32), jnp.zeros((rs, 1), F32)
            streams.append((pair[i * rs:(i + 1) * rs], slice(0, LANES), m0, l0))
    outs = _online_softmax(streams, state, kl_ref, vl_ref, kc_ref, vc_ref, n_lat, tk)
    for c in range(4):
        pair = jnp.concatenate(outs[c * per_pair:(c + 1) * per_pair], axis=0)
        o_ref[0, :, c * LANES:(c + 1) * LANES] = _unstack_pair(pair, c, tq).astype(BF16)


def _diff_kernel(lam_ref, q_ref, kl_ref, vl_ref, kc_ref, vc_ref, gain_ref, o_ref, *state, tq, tk, n_lat, out_scale):
    lane = lax.broadcasted_iota(jnp.int32, (tq, LANES), 1)
    streams = []
    for n in range(4):
        lanes = slice(n * LANES, (n + 1) * LANES)
        z = q_ref[0, :, lanes]
        zero = jnp.zeros_like(z)
        for qs in (jnp.where(lane < HEAD_DIM, z, zero), jnp.where(lane >= HEAD_DIM, z, zero)):
            streams.append((qs, lanes, jnp.full((tq, 1), NEG, F32), jnp.zeros((tq, 1), F32)))
    outs = _online_softmax(streams, state, kl_ref, vl_ref, kc_ref, vc_ref, n_lat, tk)
    for n in range(4):
        d = outs[2 * n] - lam_ref[0] * outs[2 * n + 1]
        d = d * lax.rsqrt(jnp.mean(d * d, axis=-1, keepdims=True) + EPS) * gain_ref[...] * out_scale
        o_ref[0, :, n * LANES:(n + 1) * LANES] = d.astype(BF16)


def _lane_blocks(a):
    return [a[:, k * LANES:(k + 1) * LANES] for k in range(a.shape[1] // LANES)]


def _window_kernel(sink_ref, q_ref, kl_ref, vl_ref, kc_ref, vc_ref, o_ref, *, tq, nb, length):
    span = tq + 2 * WINDOW
    kc = kc_ref[0]
    vc = vc_ref[0]
    rel = lax.broadcasted_iota(jnp.int32, (tq, span), 1) - lax.broadcasted_iota(jnp.int32, (tq, span), 0)
    chains = [(j, h) for j in range(nb) for h in range(8)]
    starts = []
    for j in range(nb):
        first = (pl.program_id(1) * nb + j) * tq
        starts.append((first, pl.multiple_of(jnp.clip(first - WINDOW, 0, length - span), LANES)))
    sw, sc = [], []
    for j in range(nb):
        first, start = starts[j]
        valid = jnp.abs(rel + (start - first)) <= WINDOW
        for c in range(4):
            pair = _stack_pair(q_ref, c, tq, j * tq)
            for qs in (pair[:tq], pair[tq:]):
                sw.append(jnp.where(valid, _qk(qs, kl_ref[0, pl.ds(start, span), :]), NEG))
                sc.append(_qk(qs, kc))
    snk = [jnp.full((tq, 1), sink_ref[h], F32) for _, h in chains]
    top = [jnp.max(functools.reduce(jnp.maximum, _lane_blocks(a) + _lane_blocks(b)), axis=-1, keepdims=True)
           for a, b in zip(sw, sc)]
    m = [jnp.maximum(t, s) for t, s in zip(top, snk)]
    pw = [jnp.exp2(a - mm) for a, mm in zip(sw, m)]
    pc = [jnp.exp2(b - mm) for b, mm in zip(sc, m)]
    tot = [jnp.sum(functools.reduce(jnp.add, _lane_blocks(a) + _lane_blocks(b)), axis=-1, keepdims=True)
           for a, b in zip(pw, pc)]
    l = [t + jnp.exp2(s - mm) for t, s, mm in zip(tot, snk, m)]
    o = [jnp.dot(a.astype(BF16), vl_ref[0, pl.ds(starts[j][1], span), :], preferred_element_type=F32)
         + jnp.dot(b.astype(BF16), vc, preferred_element_type=F32) for (j, _), a, b in zip(chains, pw, pc)]
    o = [a / ll for a, ll in zip(o, l)]
    for j in range(nb):
        for c in range(4):
            pair = jnp.concatenate([o[8 * j + 2 * c], o[8 * j + 2 * c + 1]], axis=0)
            o_ref[0, j * tq:(j + 1) * tq, c * LANES:(c + 1) * LANES] = _unstack_pair(pair, c, tq).astype(BF16)


_SMEM = pl.BlockSpec(memory_space=pltpu.SMEM)


def _kv_specs(klat, kctx, k_blk, v_blk, width):
    s_lat, s_ctx = klat.shape[1], kctx.shape[1]
    return [pl.BlockSpec((1, s_lat, width), lambda b, i: (b, 0, k_blk)),
            pl.BlockSpec((1, s_lat, width), lambda b, i: (b, 0, v_blk)),
            pl.BlockSpec((1, s_ctx, width), lambda b, i: (b, 0, k_blk)),
            pl.BlockSpec((1, s_ctx, width), lambda b, i: (b, 0, v_blk))]


def _gqa(sink, q, klat, vlat, kctx, vctx, *, q_blk, k_blk, v_blk, tq, tk, n_lat, use_sink, name):
    bx, t, _ = q.shape
    return pl.pallas_call(
        functools.partial(_gqa_kernel, tq=tq, tk=tk, n_lat=n_lat, use_sink=use_sink),
        grid=(bx, t // tq),
        in_specs=[_SMEM, pl.BlockSpec((1, tq, 512), lambda b, i: (b, i, q_blk))]
        + _kv_specs(klat, kctx, k_blk, v_blk, LANES),
        out_specs=pl.BlockSpec((1, tq, 512), lambda b, i: (b, i, 0)),
        out_shape=jax.ShapeDtypeStruct((bx, t, 512), BF16),
        scratch_shapes=_softmax_state(8 * tq // min(tq, GQA_STREAM_ROWS), min(tq, GQA_STREAM_ROWS)),
        compiler_params=_params(2),
        name=name,
    )(sink, q, klat, vlat, kctx, vctx)


def _diff(lam, q, klat, vlat, kctx, vctx, gain, *, tq, tk, n_lat, out_scale, name):
    bx, t, _ = q.shape
    return pl.pallas_call(
        functools.partial(_diff_kernel, tq=tq, tk=tk, n_lat=n_lat, out_scale=out_scale),
        grid=(bx, t // tq),
        in_specs=[_SMEM, pl.BlockSpec((1, tq, 512), lambda b, i: (b, i, 1))]
        + _kv_specs(klat, kctx, 3, 0, 512) + [pl.BlockSpec((1, LANES), lambda b, i: (0, 0))],
        out_specs=pl.BlockSpec((1, tq, 512), lambda b, i: (b, i, 0)),
        out_shape=jax.ShapeDtypeStruct((bx, t, 512), BF16),
        scratch_shapes=_softmax_state(8, tq),
        compiler_params=_params(2),
        name=name,
    )(lam, q, klat, vlat, kctx, vctx, gain)


def _window(sink, q, klat, vlat, kctx, vctx, *, tq, nb):
    bx, t, _ = q.shape
    return pl.pallas_call(
        functools.partial(_window_kernel, tq=tq, nb=nb, length=t),
        grid=(bx, t // (nb * tq)),
        in_specs=[_SMEM, pl.BlockSpec((1, nb * tq, 512), lambda b, i: (b, i, 2))]
        + _kv_specs(klat, kctx, 17, 5, LANES),
        out_specs=pl.BlockSpec((1, nb * tq, 512), lambda b, i: (b, i, 0)),
        out_shape=jax.ShapeDtypeStruct((bx, t, 512), BF16),
        compiler_params=_params(2),
        name="attn_window",
    )(sink, q, klat, vlat, kctx, vctx)


def _route(logits, tri, base):
    lane = lax.broadcasted_iota(jnp.int32, logits.shape, 1).astype(F32)
    first = lambda hit: jnp.min(jnp.where(hit, lane, float(LANES)), axis=-1, keepdims=True)
    gl = jnp.where(lane < N_GROUPS, logits, NEG)
    gmax = jnp.max(gl, axis=-1, keepdims=True)
    gidx = first(gl == gmax)
    gw = 1.0 / jnp.sum(jnp.exp(gl - gmax), axis=-1, keepdims=True)
    lo = N_GROUPS + gidx * EXPERTS_PER_GROUP
    el = jnp.where((lane >= lo) & (lane < lo + EXPERTS_PER_GROUP), logits, NEG)
    v1 = jnp.max(el, axis=-1, keepdims=True)
    i1 = first(el == v1)
    el = jnp.where(lane == i1, NEG, el)
    v2 = jnp.max(el, axis=-1, keepdims=True)
    i2 = first(el == v2)
    e = jnp.exp(v2 - v1)
    w1 = gw / (1.0 + e)
    w2 = gw * e / (1.0 + e)
    e1 = i1 - N_GROUPS
    e2 = i2 - N_GROUPS
    hit1 = jnp.where(lane == e1, 1.0, 0.0)
    hit2 = jnp.where(lane == e2, 1.0, 0.0)
    before1 = jnp.dot(tri, hit1.astype(BF16), preferred_element_type=F32) + base
    base = base + jnp.sum(hit1, axis=0, keepdims=True)
    before2 = jnp.dot(tri, hit2.astype(BF16), preferred_element_type=F32) + base
    base = base + jnp.sum(hit2, axis=0, keepdims=True)
    r1 = jnp.sum(hit1 * before1, axis=-1, keepdims=True)
    r2 = jnp.sum(hit2 * before2, axis=-1, keepdims=True)
    out = jnp.zeros_like(logits)
    for k, val in enumerate((e1, e2, w1, w2, r1, r2)):
        out = jnp.where(lane == k, val, out)
    return out, base


def _merge_kernel(ya_ref, yb_ref, yc_ref, g_ref, x_ref, gate_ref, shift_ref, scale_ref,
                  wa_ref, wb_ref, wc_ref, wo_ref, wr_ref, tri_ref, base_ref, xo_ref, h_ref, r_ref, cnt_ref):
    d = D_MODEL

    @pl.when((pl.program_id(0) == 0) & (pl.program_id(1) == 0))
    def _():
        cnt_ref[...] = base_ref[...]

    m = None
    for k, (y_ref, w_ref) in enumerate(((ya_ref, wa_ref), (yb_ref, wb_ref), (yc_ref, wc_ref))):
        t = g_ref[0, :, k * d:(k + 1) * d].astype(F32) * jnp.dot(y_ref[0], w_ref[...], preferred_element_type=F32)
        m = t if m is None else m + t
    xn = x_ref[0] + gate_ref[0] * jnp.dot(m.astype(BF16), wo_ref[...], preferred_element_type=F32)
    xo_ref[0] = xn
    h2 = _rms_modulate(xn, shift_ref[0], scale_ref[0])
    h_ref[0] = h2
    hi = h2.astype(BF16)
    lo = (h2 - hi.astype(F32)).astype(BF16)
    both = jnp.dot(hi, wr_ref[...], preferred_element_type=F32)
    logits = (both[:, :LANES] + both[:, LANES:]) + jnp.dot(lo, wr_ref[:, :LANES], preferred_element_type=F32)
    r_ref[0], cnt_ref[...] = _route(logits, tri_ref[...], cnt_ref[...])


def _merge(ya, yb, yc, g, x, gate, shift, scale, wa, wb, wc, wo, wr, base, *, tm, name):
    bx, t, d = x.shape
    row = lambda b, i: (b, 0, 0)
    tile = lambda b, i: (b, i, 0)
    ids = jnp.arange(tm, dtype=jnp.int32)
    tri = (ids[None, :] < ids[:, None]).astype(BF16)
    return pl.pallas_call(
        _merge_kernel,
        grid=(bx, t // tm),
        in_specs=[pl.BlockSpec((1, tm, 512), tile)] * 3
        + [pl.BlockSpec((1, tm, G_COLS), tile), pl.BlockSpec((1, tm, d), tile)]
        + [pl.BlockSpec((1, 1, d), row)] * 3
        + [_resident((512, d))] * 3 + [_resident((d, d)), _resident((d, 2 * LANES)), _resident((tm, tm)),
                                      _resident((1, LANES))],
        out_specs=[pl.BlockSpec((1, tm, d), tile), pl.BlockSpec((1, tm, d), tile), pl.BlockSpec((1, tm, LANES), tile),
                   pl.BlockSpec((1, LANES), lambda b, i: (0, 0))],
        out_shape=[jax.ShapeDtypeStruct((bx, t, d), F32), jax.ShapeDtypeStruct((bx, t, d), F32),
                   jax.ShapeDtypeStruct((bx, t, LANES), F32), jax.ShapeDtypeStruct((1, LANES), F32)],
        compiler_params=_params(2),
        name=name,
    )(ya, yb, yc, g, x, gate, shift, scale, wa, wb, wc, wo, wr, tri, base)


def _dispatch_kernel(dest_ref, h_ref, xs_in_ref, xs_ref, sem, *, tokens):
    del xs_in_ref

    def row_copy(r, d):
        return pltpu.make_async_copy(h_ref.at[pl.ds(r, 1), :], xs_ref.at[pl.ds(d, 1), :], sem)

    def issue(r, carry):
        for k in range(2):
            row_copy(r, dest_ref[0, 0, 2 * r + k]).start(priority=k)
        return carry

    lax.fori_loop(0, tokens, issue, 0, unroll=8)
    for _ in range(2):
        pltpu.make_async_copy(h_ref, xs_ref.at[pl.ds(0, tokens), :], sem).wait()


def _dispatch(dest, h, xs, *, td):
    n, d = h.shape
    return pl.pallas_call(
        functools.partial(_dispatch_kernel, tokens=td),
        grid=(n // td,),
        in_specs=[pl.BlockSpec((1, 1, 2 * td), lambda i: (i, 0, 0), memory_space=pltpu.SMEM),
                  pl.BlockSpec((td, d), lambda i: (i, 0)),
                  pl.BlockSpec(memory_space=pl.ANY)],
        out_specs=pl.BlockSpec(memory_space=pl.ANY),
        out_shape=jax.ShapeDtypeStruct(xs.shape, xs.dtype),
        scratch_shapes=[pltpu.SemaphoreType.DMA(())],
        input_output_aliases={2: 0},
        compiler_params=_params(1, row_dma=True),
        name="moe_dispatch",
    )(dest.reshape(n // td, 1, 2 * td), h, xs)


def _experts_kernel(te_ref, used_ref, xs_ref, wg_ref, wu_ref, wd_ref, y_ref, wg_bf, wu_bf, wd_bf):
    t = pl.program_id(0)
    live = t < used_ref[0]

    @pl.when(live & ((t == 0) | (te_ref[t] != te_ref[jnp.maximum(t - 1, 0)])))
    def _():
        wg_bf[...] = wg_ref[0, 0].astype(BF16)
        wu_bf[...] = wu_ref[0, 0].astype(BF16)
        wd_bf[...] = wd_ref[0, 0].astype(BF16)

    @pl.when(live)
    def _():
        xb = xs_ref[...].astype(BF16)
        a = jnp.dot(xb, wg_bf[...], preferred_element_type=F32)
        u = jnp.dot(xb, wu_bf[...], preferred_element_type=F32)
        y_ref[...] = jnp.dot((a * _sigmoid(a) * u).astype(BF16), wd_bf[...], preferred_element_type=F32)

    @pl.when(t >= used_ref[0])
    def _():
        y_ref[...] = jnp.zeros_like(y_ref)


def _experts(tile_expert, n_used, xs, wg, wu, wd, *, layer, tm):
    p, d = xs.shape
    hid = wg.shape[3]
    weights = lambda t, te, nu: (layer, te[t], 0, 0)
    return pl.pallas_call(
        _experts_kernel,
        grid_spec=pltpu.PrefetchScalarGridSpec(
            num_scalar_prefetch=2,
            grid=(p // tm,),
            in_specs=[pl.BlockSpec((tm, d), lambda t, te, nu: (t, 0)),
                      pl.BlockSpec((1, 1, d, hid), weights), pl.BlockSpec((1, 1, d, hid), weights),
                      pl.BlockSpec((1, 1, hid, d), weights)],
            out_specs=pl.BlockSpec((tm, d), lambda t, te, nu: (t, 0)),
            scratch_shapes=[pltpu.VMEM((d, hid), BF16), pltpu.VMEM((d, hid), BF16), pltpu.VMEM((hid, d), BF16)]),
        out_shape=jax.ShapeDtypeStruct((p, d), F32),
        compiler_params=_params(1),
        name="moe_experts",
    )(tile_expert, n_used, xs, wg, wu, wd)


def _combine_kernel(dest_ref, y_ref, x_ref, gate_ref, r_ref, o_ref, buf, sem, *, tokens):
    def row_copy(r, k, d):
        return pltpu.make_async_copy(y_ref.at[pl.ds(d, 1), :], buf.at[k, pl.ds(r, 1), :], sem)

    def issue(r, carry):
        for k in range(2):
            row_copy(r, k, dest_ref[0, 0, 2 * r + k]).start(priority=k)
        return carry

    lax.fori_loop(0, tokens, issue, 0, unroll=8)
    for k in range(2):
        pltpu.make_async_copy(y_ref.at[pl.ds(0, tokens), :], buf.at[k], sem).wait()
    w = r_ref[...]
    o_ref[0] = x_ref[0] + gate_ref[0] * (w[:, 2:3] * buf[0] + w[:, 3:4] * buf[1])


def _combine(dest, y, x, gate, route, *, tc, name):
    bx, t, d = x.shape
    per = t // tc
    return pl.pallas_call(
        functools.partial(_combine_kernel, tokens=tc),
        grid=(bx * per,),
        in_specs=[pl.BlockSpec((1, 1, 2 * tc), lambda i: (i, 0, 0), memory_space=pltpu.SMEM),
                  pl.BlockSpec(memory_space=pl.ANY),
                  pl.BlockSpec((1, tc, d), lambda i: (i // per, i % per, 0)),
                  pl.BlockSpec((1, 1, d), lambda i: (i // per, 0, 0)),
                  pl.BlockSpec((tc, LANES), lambda i: (i, 0))],
        out_specs=pl.BlockSpec((1, tc, d), lambda i: (i // per, i % per, 0)),
        out_shape=jax.ShapeDtypeStruct(x.shape, F32),
        scratch_shapes=[pltpu.VMEM((2, tc, d), F32), pltpu.SemaphoreType.DMA(())],
        compiler_params=_params(1, row_dma=True),
        name=name,
    )(dest.reshape(bx * per, 1, 2 * tc), y, x, gate, route)


def _expert_slots(routes, counts, tm):
    n_assign = 2 * sum(r.shape[0] for r in routes)
    padded = ((counts + tm - 1) // tm) * tm
    ends = jnp.cumsum(padded)
    starts = ends - padded
    experts = jnp.arange(N_EXPERTS, dtype=jnp.int32)
    dests = []
    for r in routes:
        e = r[:, 0:2].astype(jnp.int32)
        rank = r[:, 4:6].astype(jnp.int32)
        start = jnp.sum(jnp.where(e[:, :, None] == experts, starts, 0), axis=-1)
        dests.append((start + rank).reshape(-1))
    n_tiles = n_assign // tm + N_EXPERTS
    tiles = jnp.arange(n_tiles, dtype=jnp.int32)
    tile_expert = jnp.minimum(jnp.sum((tiles[:, None] >= (ends // tm)[None, :]).astype(jnp.int32), axis=1),
                              N_EXPERTS - 1)
    return dests, tile_expert, (ends[-1:] // tm).astype(jnp.int32), n_tiles


def _rope_tables(length):
    pairs = HEAD_DIM // 4
    pos = jnp.arange(length, dtype=jnp.int32)
    row = (pos // GRID_W).astype(F32)
    col = (pos % GRID_W).astype(F32)
    freqs = ROPE_THETA ** (-jnp.arange(pairs, dtype=F32) / pairs)
    ang = jnp.concatenate([row[:, None] * freqs, col[:, None] * freqs], axis=-1)
    cos = jnp.tile(jnp.cos(ang), (1, 4))
    sin = jnp.tile(jnp.concatenate([-jnp.sin(ang), jnp.sin(ang)], axis=-1), (1, 2))
    return cos, sin


def kernel(x, c, ctx, c_ctx, w_ada, b_ada, w_in, a_qnorm, a_knorm, b_qnorm, b_knorm, c_qnorm, c_knorm, lambda_q1, lambda_k1, lambda_q2, lambda_k2, b_subln, c_sink, w_branch_a, w_branch_b, w_branch_c, w_out, w_router_group, w_router_expert, w_exp_gate, w_exp_up, w_exp_down):
    bsz, length, d = x.shape
    c_len = ctx.shape[1]
    depth = w_ada.shape[0]
    assert d == D_MODEL and bsz + 1 <= MOD_ROWS and length % 512 == 0 and c_len == 256

    cos, sin = _rope_tables(length)
    no_rope = jnp.zeros((c_len, LANES), F32)
    head_ids = jnp.arange(NORM_CHUNK, dtype=jnp.int32) // HEAD_DIM
    ones = (head_ids[:, None] == head_ids[None, :]).astype(BF16)
    c_rows = jnp.concatenate([c, c_ctx[None, :], jnp.zeros((MOD_ROWS - bsz - 1, d), F32)], axis=0)
    mod_all = _ada(c_rows, w_ada, b_ada.reshape(depth, 1, -1))
    zero_sink = jnp.zeros((8,), F32)

    tq_a, tq_b = 256, 256
    tk = 2048 if length % 2048 == 0 else 512
    n_lat = length // tk
    tm_moe = 512

    for l in range(depth):
        need_ctx = l < depth - 1
        mod = lambda k: mod_all[l, :bsz, k * d:(k + 1) * d].reshape(bsz, 1, d)
        cmod = lambda k: jnp.broadcast_to(mod_all[l, bsz, k * d:(k + 1) * d], (bsz, 1, d))
        lam_init = 0.8 - 0.6 * math.exp(-0.3 * l)
        lam = (jnp.exp(jnp.sum(lambda_q1[l] * lambda_k1[l])) - jnp.exp(jnp.sum(lambda_q2[l] * lambda_k2[l]))
               + lam_init).reshape(1).astype(F32)

        w = w_in[l]
        wqk = jnp.concatenate([w[:, 0:1536], w[:, 4864:5376], w[:, 4608:4736], w[:, 5888:6016]], axis=1).astype(BF16)
        wg = w[:, 1536:4608].astype(BF16)
        wv = jnp.concatenate([w[:, 5376:5888], w[:, 4736:4864], w[:, 6016:6144]], axis=1).astype(BF16)
        q_scale = HEAD_DIM ** -0.5 * LOG2E
        gain = jnp.concatenate([jnp.tile(a_qnorm[l], 8) * q_scale, jnp.tile(b_qnorm[l], 8) * q_scale,
                                jnp.tile(c_qnorm[l], 8) * q_scale, jnp.tile(b_knorm[l], 8),
                                jnp.tile(a_knorm[l], 2), jnp.tile(c_knorm[l], 2)]).reshape(1, QK_COLS)
        subln = b_subln[l].reshape(1, LANES)
        sink = c_sink[l] * LOG2E
        w_router = jnp.concatenate([w_router_group[l], w_router_expert[l],
                                    jnp.zeros((d, LANES - N_GROUPS - N_EXPERTS), F32)], axis=1)
        w_router_hi = w_router.astype(BF16)
        w_router_lo = (w_router - w_router_hi.astype(F32)).astype(BF16)
        merge_w = (w_branch_a[l].astype(BF16), w_branch_b[l].astype(BF16), w_branch_c[l].astype(BF16),
                   w_out[l].astype(BF16), jnp.concatenate([w_router_hi, w_router_lo], axis=1))

        qk, gates, v = _inproj(x, mod(0), mod(1), wqk, wg, wv, gain, ones, cos, sin, use_rope=True, tm=512)
        cqk, cgates, cv = _inproj(ctx, cmod(0), cmod(1), wqk, wg, wv, gain, ones, no_rope, no_rope,
                                  use_rope=False, tm=c_len)
        ya = _gqa(zero_sink, qk, qk, v, cqk, cv, q_blk=0, k_blk=16, v_blk=4, tq=tq_a, tk=tk, n_lat=n_lat,
                  use_sink=False, name="attn_gqa")
        yb = _diff(lam, qk, qk, v, cqk, cv, subln, tq=tq_b, tk=tk, n_lat=n_lat, out_scale=1.0 - lam_init,
                   name="attn_diff")
        yc = _window(sink, qk, qk, v, cqk, cv, tq=128, nb=4)
        x, h2, route, counts = _merge(ya, yb, yc, gates, x, mod(2), mod(3), mod(4), *merge_w,
                                      jnp.zeros((1, LANES), F32), tm=512, name="merge")
        tokens = [h2.reshape(bsz * length, d)]
        routes = [route.reshape(bsz * length, LANES)]
        if need_ctx:
            cya = _gqa(zero_sink, cqk, cqk, cv, cqk, cv, q_blk=0, k_blk=16, v_blk=4, tq=128, tk=tk, n_lat=0,
                       use_sink=False, name="ctx_gqa")
            cyb = _diff(lam, cqk, cqk, cv, cqk, cv, subln, tq=c_len, tk=tk, n_lat=0, out_scale=1.0 - lam_init,
                        name="ctx_diff")
            cyc = _gqa(sink, cqk, cqk, cv, cqk, cv, q_blk=2, k_blk=17, v_blk=5, tq=128, tk=tk, n_lat=0,
                       use_sink=True, name="ctx_sink")
            ctx, hc2, croute, counts = _merge(cya, cyb, cyc, cgates, ctx, cmod(2), cmod(3), cmod(4), *merge_w, counts,
                                              tm=c_len, name="ctx_merge")
            tokens.append(hc2.reshape(bsz * c_len, d))
            routes.append(croute.reshape(bsz * c_len, LANES))

        dests, tile_expert, n_used, n_tiles = _expert_slots(routes, counts[0, :N_EXPERTS].astype(jnp.int32), tm_moe)
        xs = jnp.zeros((n_tiles * tm_moe, d), F32)
        for t, dest in zip(tokens, dests):
            xs = _dispatch(dest, t, xs, td=256)
        y = _experts(tile_expert, n_used, xs, w_exp_gate, w_exp_up, w_exp_down, layer=l, tm=tm_moe)
        x = _combine(dests[0], y, x, mod(5), routes[0], tc=256, name="moe_combine")
        if need_ctx:
            ctx = _combine(dests[1], y, ctx, cmod(5), routes[1], tc=256, name="ctx_combine")
    return x
```

```python
import functools
import math

import jax
import jax.numpy as jnp
from jax import lax
from jax.experimental import pallas as pl
from jax.experimental.pallas import tpu as pltpu

F32 = jnp.float32
BF16 = jnp.bfloat16

D_MODEL = 1024
HEAD_DIM = 64
GRID_W = 64
ROPE_THETA = 10000.0
WINDOW = 128
N_GROUPS = 4
EXPERTS_PER_GROUP = 8
N_EXPERTS = N_GROUPS * EXPERTS_PER_GROUP
EXPERT_HIDDEN = D_MODEL // 2
N_MOD = 6
EPS = 1e-6
NEG = -1e30
LOG2E = math.log2(math.e)
LANES = 128
MOD_ROWS = 16

QK_COLS = 2304
G_COLS = 3 * D_MODEL
V_COLS = 768
MIXER_WIDTH = 512
W_IN_COLS = {"a_q": (0, 512), "b_q": (512, 1024), "c_q": (1024, 1536), "gates": (1536, 4608), "a_k": (4608, 4736),
             "a_v": (4736, 4864), "b_k": (4864, 5376), "b_v": (5376, 5888), "c_k": (5888, 6016), "c_v": (6016, 6144)}
QK_ORDER = ("a_q", "b_q", "c_q", "b_k", "a_k", "c_k")
V_ORDER = ("b_v", "a_v", "c_v")


def _offsets(order):
    out, at = {}, 0
    for name in order:
        out[name] = at
        at += W_IN_COLS[name][1] - W_IN_COLS[name][0]
    return out


QK_OFFSET = _offsets(QK_ORDER)
V_OFFSET = _offsets(V_ORDER)
NORM_CHUNK = 256
SUB = 256
GQA_STREAM_ROWS = 256

VMEM_LIMIT = 48 * 1024 * 1024


def _params(n_axes, vmem=VMEM_LIMIT, row_dma=False):
    return pltpu.CompilerParams(dimension_semantics=("arbitrary",) * n_axes, vmem_limit_bytes=vmem,
                                disable_bounds_checks=row_dma)


def _resident(shape):
    return pl.BlockSpec(shape, lambda *_: (0,) * len(shape), pipeline_mode=pl.Buffered(1))


def _sigmoid(v):
    return 1.0 / (1.0 + jnp.exp(-v))


def _rms_modulate(v, shift, scale):
    v = v * lax.rsqrt(jnp.mean(v * v, axis=-1, keepdims=True) + EPS)
    return v * (1.0 + scale) + shift


def _ada_kernel(c_ref, w_ref, b_ref, o_ref):
    c = c_ref[...]
    o_ref[0] = jnp.dot(c * _sigmoid(c), w_ref[0], preferred_element_type=F32,
                       precision=lax.Precision.HIGHEST) + b_ref[0]


def _ada(c_rows, w_ada, b_ada):
    depth, d, n = w_ada.shape
    tn = 1536
    return pl.pallas_call(
        _ada_kernel,
        grid=(depth, n // tn),
        in_specs=[pl.BlockSpec((MOD_ROWS, d), lambda l, j: (0, 0)),
                  pl.BlockSpec((1, d, tn), lambda l, j: (l, 0, j)),
                  pl.BlockSpec((1, 1, tn), lambda l, j: (l, 0, j))],
        out_specs=pl.BlockSpec((1, MOD_ROWS, tn), lambda l, j: (l, 0, j)),
        out_shape=jax.ShapeDtypeStruct((depth, MOD_ROWS, n), F32),
        compiler_params=_params(2),
        name="ada",
    )(c_rows, w_ada, b_ada)


def _rope_partner(z):
    lane = lax.broadcasted_iota(jnp.int32, z.shape, 1)
    return jnp.where((lane & 32) == 0, pltpu.roll(z, 96, 1), pltpu.roll(z, 32, 1))


def _inproj_kernel(x_ref, shift_ref, scale_ref, wqk_ref, wg_ref, wv_ref, gain_ref, ones_ref, cos_ref, sin_ref,
                   oqk_ref, og_ref, ov_ref, *, use_rope):
    hb = _rms_modulate(x_ref[0], shift_ref[0], scale_ref[0]).astype(BF16)
    chunks = [slice(c * NORM_CHUNK, (c + 1) * NORM_CHUNK) for c in range(QK_COLS // NORM_CHUNK)]
    ys = [jnp.dot(hb, wqk_ref[:, cols], preferred_element_type=F32) for cols in chunks]
    sss = [jnp.dot((y * y).astype(BF16), ones_ref[...], preferred_element_type=F32) for y in ys]
    for c in range(G_COLS // 512):
        cols = slice(c * 512, (c + 1) * 512)
        og_ref[0, :, cols] = _sigmoid(jnp.dot(hb, wg_ref[:, cols], preferred_element_type=F32)).astype(BF16)
    ov_ref[0] = jnp.dot(hb, wv_ref[...], preferred_element_type=F32).astype(BF16)
    for cols, y, ss in zip(chunks, ys, sss):
        yn = y * lax.rsqrt(ss * (1.0 / HEAD_DIM) + EPS) * gain_ref[:, cols]
        if use_rope:
            halves = []
            for k in range(NORM_CHUNK // LANES):
                z = yn[:, k * LANES:(k + 1) * LANES]
                halves.append(z * cos_ref[...] + _rope_partner(z) * sin_ref[...])
            yn = jnp.concatenate(halves, axis=1)
        oqk_ref[0, :, cols] = yn.astype(BF16)


def _inproj(x, shift, scale, wqk, wg, wv, gain, ones, cos, sin, *, use_rope, tm):
    bx, t, d = x.shape
    row = lambda b, i: (b, 0, 0)
    tile = lambda b, i: (b, i, 0)
    return pl.pallas_call(
        functools.partial(_inproj_kernel, use_rope=use_rope),
        grid=(bx, t // tm),
        in_specs=[pl.BlockSpec((1, tm, d), tile),
                  pl.BlockSpec((1, 1, d), row), pl.BlockSpec((1, 1, d), row),
                  _resident((d, QK_COLS)), _resident((d, G_COLS)), _resident((d, V_COLS)),
                  _resident((1, QK_COLS)), _resident((NORM_CHUNK, NORM_CHUNK)),
                  pl.BlockSpec((tm, LANES), lambda b, i: (i, 0)), pl.BlockSpec((tm, LANES), lambda b, i: (i, 0))],
        out_specs=[pl.BlockSpec((1, tm, QK_COLS), tile), pl.BlockSpec((1, tm, G_COLS), tile),
                   pl.BlockSpec((1, tm, V_COLS), tile)],
        out_shape=[jax.ShapeDtypeStruct((bx, t, QK_COLS), BF16), jax.ShapeDtypeStruct((bx, t, G_COLS), BF16),
                   jax.ShapeDtypeStruct((bx, t, V_COLS), BF16)],
        compiler_params=_params(2),
        name="inproj_rope" if use_rope else "inproj_ctx",
    )(x, shift, scale, wqk, wg, wv, gain, ones, cos, sin)


def _qk(qs, k):
    return lax.dot_general(qs, k, (((1,), (1,)), ((), ())), preferred_element_type=F32)


def _online_softmax(streams, state, kl_ref, vl_ref, kc_ref, vc_ref, n_lat, tk):
    n = len(streams)
    state = [state[4 * i:4 * i + 4] for i in range(n)]
    for (qs, _, m0, l0), (qs_ref, m_ref, l_ref, acc_ref) in zip(streams, state):
        lane = lax.broadcasted_iota(jnp.int32, m_ref.shape, 1)
        qs_ref[...] = qs
        m_ref[...] = jnp.broadcast_to(m0, m_ref.shape)
        l_ref[...] = jnp.where(lane == 0, l0, 0.0)
        acc_ref[...] = jnp.zeros(acc_ref.shape, F32)

    def scores(i, key_block, width):
        qs = state[i][0][...]
        return [_qk(qs, key_block(c)) for c in range(width // SUB)]

    def update(i, subs, value_block):
        _, m_ref, l_ref, acc_ref = state[i]
        m = m_ref[...]
        top = subs[0]
        for s in subs[1:]:
            top = jnp.maximum(top, s)
        top = functools.reduce(jnp.maximum, [top[:, k * LANES:(k + 1) * LANES] for k in range(SUB // LANES)])
        m_new = jnp.maximum(m, jnp.max(top, axis=-1, keepdims=True))
        alpha = jnp.exp2(m - m_new)
        m_wide = jnp.concatenate([m_new] * (SUB // LANES), axis=1)
        l = alpha * l_ref[...]
        acc = alpha * acc_ref[...]
        for c, s in enumerate(subs):
            p = jnp.exp2(s - m_wide)
            for k in range(SUB // LANES):
                l = l + p[:, k * LANES:(k + 1) * LANES]
            acc = acc + jnp.dot(p.astype(BF16), value_block(c), preferred_element_type=F32)
        m_ref[...] = m_new
        l_ref[...] = l
        acc_ref[...] = acc

    def step(key_block, value_block, width):
        subs = [scores(i, functools.partial(key_block, lanes), width) for i, (_, lanes, _, _) in enumerate(streams)]
        for i, (_, lanes, _, _) in enumerate(streams):
            update(i, subs[i], functools.partial(value_block, lanes))

    if n_lat:
        def body(j, carry):
            rows = lambda c: pl.ds(pl.multiple_of(j * tk + c * SUB, SUB), SUB)
            step(lambda lanes, c: kl_ref[0, rows(c), lanes], lambda lanes, c: vl_ref[0, rows(c), lanes], tk)
            return carry
        lax.fori_loop(0, n_lat, body, 0)
    step(lambda lanes, c: kc_ref[0, c * SUB:(c + 1) * SUB, lanes],
         lambda lanes, c: vc_ref[0, c * SUB:(c + 1) * SUB, lanes], kc_ref.shape[1])
    return [acc_ref[...] / jnp.sum(l_ref[...], axis=-1, keepdims=True) for _, _, l_ref, acc_ref in state]


def _softmax_state(n_streams, rows):
    return [pltpu.VMEM((rows, LANES), BF16), pltpu.VMEM((rows, LANES), F32), pltpu.VMEM((rows, LANES), F32),
            pltpu.VMEM((rows, LANES), F32)] * n_streams


def _stack_pair(q_ref, c, tq, row0=0):
    g = c // 2
    lane = lax.broadcasted_iota(jnp.int32, (tq, LANES), 1)
    keep = (lane >= HEAD_DIM) if g else (lane < HEAD_DIM)
    z = q_ref[0, row0:row0 + tq, c * LANES:(c + 1) * LANES].astype(F32)
    swapped = pltpu.roll(z, HEAD_DIM, 1)
    first, second = (z, swapped) if g == 0 else (swapped, z)
    return jnp.concatenate([jnp.where(keep, first, 0.0).astype(BF16), jnp.where(keep, second, 0.0).astype(BF16)], axis=0)


def _unstack_pair(o, c, tq):
    lane = lax.broadcasted_iota(jnp.int32, (tq, LANES), 1)
    a, b = o[:tq], o[tq:]
    if c // 2 == 0:
        b = pltpu.roll(b, HEAD_DIM, 1)
    else:
        a = pltpu.roll(a, HEAD_DIM, 1)
    return jnp.where(lane < HEAD_DIM, a, b)


def _sink_rows(sink_ref, heads, tq):
    return jnp.concatenate([jnp.full((tq, 1), sink_ref[h], F32) for h in heads], axis=0)


def _gqa_kernel(sink_ref, q_ref, kl_ref, vl_ref, kc_ref, vc_ref, o_ref, *state, tq, tk, n_lat, use_sink):
    rs = min(tq, GQA_STREAM_ROWS)
    per_pair = 2 * tq // rs
    streams = []
    for c in range(4):
        pair = _stack_pair(q_ref, c, tq)
        for i in range(per_pair):
            h = 2 * c + i * rs // tq
            if use_sink:
                m0, l0 = jnp.full((rs, 1), sink_ref[h], F32), jnp.ones((rs, 1), F32)
            else:
                m0, l0 = jnp.full((rs, 1), NEG, F32), jnp.zeros((rs, 1), F32)
            streams.append((pair[i * rs:(i + 1) * rs], slice(0, LANES), m0, l0))
    outs = _online_softmax(streams, state, kl_ref, vl_ref, kc_ref, vc_ref, n_lat, tk)
    for c in range(4):
        pair = jnp.concatenate(outs[c * per_pair:(c + 1) * per_pair], axis=0)
        o_ref[0, :, c * LANES:(c + 1) * LANES] = _unstack_pair(pair, c, tq).astype(BF16)


def _diff_kernel(lam_ref, q_ref, kl_ref, vl_ref, kc_ref, vc_ref, gain_ref, o_ref, *state, tq, tk, n_lat, out_scale):
    lane = lax.broadcasted_iota(jnp.int32, (tq, LANES), 1)
    streams = []
    for n in range(4):
        lanes = slice(n * LANES, (n + 1) * LANES)
        z = q_ref[0, :, lanes]
        zero = jnp.zeros_like(z)
        for qs in (jnp.where(lane < HEAD_DIM, z, zero), jnp.where(lane >= HEAD_DIM, z, zero)):
            streams.append((qs, lanes, jnp.full((tq, 1), NEG, F32), jnp.zeros((tq, 1), F32)))
    outs = _online_softmax(streams, state, kl_ref, vl_ref, kc_ref, vc_ref, n_lat, tk)
    for n in range(4):
        d = outs[2 * n] - lam_ref[0] * outs[2 * n + 1]
        d = d * lax.rsqrt(jnp.mean(d * d, axis=-1, keepdims=True) + EPS) * gain_ref[...] * out_scale
        o_ref[0, :, n * LANES:(n + 1) * LANES] = d.astype(BF16)


def _lane_blocks(a):
    return [a[:, k * LANES:(k + 1) * LANES] for k in range(a.shape[1] // LANES)]


def _window_kernel(sink_ref, q_ref, kl_ref, vl_ref, kc_ref, vc_ref, o_ref, *, tq, nb, length):
    span = tq + 2 * WINDOW
    kc = kc_ref[0]
    vc = vc_ref[0]
    rel = lax.broadcasted_iota(jnp.int32, (tq, span), 1) - lax.broadcasted_iota(jnp.int32, (tq, span), 0)
    chains = [(j, h) for j in range(nb) for h in range(8)]
    starts = []
    for j in range(nb):
        first = (pl.program_id(1) * nb + j) * tq
        starts.append((first, pl.multiple_of(jnp.clip(first - WINDOW, 0, length - span), LANES)))
    sw, sc = [], []
    for j in range(nb):
        first, start = starts[j]
        valid = jnp.abs(rel + (start - first)) <= WINDOW
        for c in range(4):
            pair = _stack_pair(q_ref, c, tq, j * tq)
            for qs in (pair[:tq], pair[tq:]):
                sw.append(jnp.where(valid, _qk(qs, kl_ref[0, pl.ds(start, span), :]), NEG))
                sc.append(_qk(qs, kc))
    snk = [jnp.full((tq, 1), sink_ref[h], F32) for _, h in chains]
    top = [jnp.max(functools.reduce(jnp.maximum, _lane_blocks(a) + _lane_blocks(b)), axis=-1, keepdims=True)
           for a, b in zip(sw, sc)]
    m = [jnp.maximum(t, s) for t, s in zip(top, snk)]
    pw = [jnp.exp2(a - mm) for a, mm in zip(sw, m)]
    pc = [jnp.exp2(b - mm) for b, mm in zip(sc, m)]
    tot = [jnp.sum(functools.reduce(jnp.add, _lane_blocks(a) + _lane_blocks(b)), axis=-1, keepdims=True)
           for a, b in zip(pw, pc)]
    l = [t + jnp.exp2(s - mm) for t, s, mm in zip(tot, snk, m)]
    o = [jnp.dot(a.astype(BF16), vl_ref[0, pl.ds(starts[j][1], span), :], preferred_element_type=F32)
         + jnp.dot(b.astype(BF16), vc, preferred_element_type=F32) for (j, _), a, b in zip(chains, pw, pc)]
    o = [a / ll for a, ll in zip(o, l)]
    for j in range(nb):
        for c in range(4):
            pair = jnp.concatenate([o[8 * j + 2 * c], o[8 * j + 2 * c + 1]], axis=0)
            o_ref[0, j * tq:(j + 1) * tq, c * LANES:(c + 1) * LANES] = _unstack_pair(pair, c, tq).astype(BF16)


_SMEM = pl.BlockSpec(memory_space=pltpu.SMEM)


def _q_spec(mixer, rows):
    blk = QK_OFFSET[mixer + "_q"] // MIXER_WIDTH
    return pl.BlockSpec((1, rows, MIXER_WIDTH), lambda b, i: (b, i, blk))


def _kv_specs(klat, kctx, mixer, width):
    s_lat, s_ctx = klat.shape[1], kctx.shape[1]
    k_blk, v_blk = QK_OFFSET[mixer + "_k"] // width, V_OFFSET[mixer + "_v"] // width
    return [pl.BlockSpec((1, s_lat, width), lambda b, i: (b, 0, k_blk)),
            pl.BlockSpec((1, s_lat, width), lambda b, i: (b, 0, v_blk)),
            pl.BlockSpec((1, s_ctx, width), lambda b, i: (b, 0, k_blk)),
            pl.BlockSpec((1, s_ctx, width), lambda b, i: (b, 0, v_blk))]


def _gqa(sink, q, klat, vlat, kctx, vctx, *, mixer, tq, tk, n_lat, use_sink, name):
    bx, t, _ = q.shape
    return pl.pallas_call(
        functools.partial(_gqa_kernel, tq=tq, tk=tk, n_lat=n_lat, use_sink=use_sink),
        grid=(bx, t // tq),
        in_specs=[_SMEM, _q_spec(mixer, tq)] + _kv_specs(klat, kctx, mixer, LANES),
        out_specs=pl.BlockSpec((1, tq, MIXER_WIDTH), lambda b, i: (b, i, 0)),
        out_shape=jax.ShapeDtypeStruct((bx, t, MIXER_WIDTH), BF16),
        scratch_shapes=_softmax_state(8 * tq // min(tq, GQA_STREAM_ROWS), min(tq, GQA_STREAM_ROWS)),
        compiler_params=_params(2),
        name=name,
    )(sink, q, klat, vlat, kctx, vctx)


def _diff(lam, q, klat, vlat, kctx, vctx, gain, *, tq, tk, n_lat, out_scale, name):
    bx, t, _ = q.shape
    return pl.pallas_call(
        functools.partial(_diff_kernel, tq=tq, tk=tk, n_lat=n_lat, out_scale=out_scale),
        grid=(bx, t // tq),
        in_specs=[_SMEM, _q_spec("b", tq)] + _kv_specs(klat, kctx, "b", MIXER_WIDTH)
        + [pl.BlockSpec((1, LANES), lambda b, i: (0, 0))],
        out_specs=pl.BlockSpec((1, tq, MIXER_WIDTH), lambda b, i: (b, i, 0)),
        out_shape=jax.ShapeDtypeStruct((bx, t, MIXER_WIDTH), BF16),
        scratch_shapes=_softmax_state(8, tq),
        compiler_params=_params(2),
        name=name,
    )(lam, q, klat, vlat, kctx, vctx, gain)


def _window(sink, q, klat, vlat, kctx, vctx, *, tq, nb):
    bx, t, _ = q.shape
    return pl.pallas_call(
        functools.partial(_window_kernel, tq=tq, nb=nb, length=t),
        grid=(bx, t // (nb * tq)),
        in_specs=[_SMEM, _q_spec("c", nb * tq)] + _kv_specs(klat, kctx, "c", LANES),
        out_specs=pl.BlockSpec((1, nb * tq, MIXER_WIDTH), lambda b, i: (b, i, 0)),
        out_shape=jax.ShapeDtypeStruct((bx, t, MIXER_WIDTH), BF16),
        compiler_params=_params(2),
        name="attn_window",
    )(sink, q, klat, vlat, kctx, vctx)


def _route(logits, tri, base):
    lane = lax.broadcasted_iota(jnp.int32, logits.shape, 1).astype(F32)
    first = lambda hit: jnp.min(jnp.where(hit, lane, float(LANES)), axis=-1, keepdims=True)
    gl = jnp.where(lane < N_GROUPS, logits, NEG)
    gmax = jnp.max(gl, axis=-1, keepdims=True)
    gidx = first(gl == gmax)
    gw = 1.0 / jnp.sum(jnp.exp(gl - gmax), axis=-1, keepdims=True)
    lo = N_GROUPS + gidx * EXPERTS_PER_GROUP
    el = jnp.where((lane >= lo) & (lane < lo + EXPERTS_PER_GROUP), logits, NEG)
    v1 = jnp.max(el, axis=-1, keepdims=True)
    i1 = first(el == v1)
    el = jnp.where(lane == i1, NEG, el)
    v2 = jnp.max(el, axis=-1, keepdims=True)
    i2 = first(el == v2)
    e = jnp.exp(v2 - v1)
    w1 = gw / (1.0 + e)
    w2 = gw * e / (1.0 + e)
    e1 = i1 - N_GROUPS
    e2 = i2 - N_GROUPS
    hit1 = jnp.where(lane == e1, 1.0, 0.0)
    hit2 = jnp.where(lane == e2, 1.0, 0.0)
    before1 = jnp.dot(tri, hit1.astype(BF16), preferred_element_type=F32) + base
    base = base + jnp.sum(hit1, axis=0, keepdims=True)
    before2 = jnp.dot(tri, hit2.astype(BF16), preferred_element_type=F32) + base
    base = base + jnp.sum(hit2, axis=0, keepdims=True)
    r1 = jnp.sum(hit1 * before1, axis=-1, keepdims=True)
    r2 = jnp.sum(hit2 * before2, axis=-1, keepdims=True)
    out = jnp.zeros_like(logits)
    for k, val in enumerate((e1, e2, w1, w2, r1, r2)):
        out = jnp.where(lane == k, val, out)
    return out, base


def _merge_kernel(ya_ref, yb_ref, yc_ref, g_ref, x_ref, gate_ref, shift_ref, scale_ref,
                  wa_ref, wb_ref, wc_ref, wo_ref, wr_ref, tri_ref, base_ref, xo_ref, h_ref, r_ref, cnt_ref):
    d = D_MODEL

    @pl.when((pl.program_id(0) == 0) & (pl.program_id(1) == 0))
    def _():
        cnt_ref[...] = base_ref[...]

    m = None
    for k, (y_ref, w_ref) in enumerate(((ya_ref, wa_ref), (yb_ref, wb_ref), (yc_ref, wc_ref))):
        t = g_ref[0, :, k * d:(k + 1) * d].astype(F32) * jnp.dot(y_ref[0], w_ref[...], preferred_element_type=F32)
        m = t if m is None else m + t
    xn = x_ref[0] + gate_ref[0] * jnp.dot(m.astype(BF16), wo_ref[...], preferred_element_type=F32)
    xo_ref[0] = xn
    h2 = _rms_modulate(xn, shift_ref[0], scale_ref[0])
    h_ref[0] = h2
    hi = h2.astype(BF16)
    lo = (h2 - hi.astype(F32)).astype(BF16)
    both = jnp.dot(hi, wr_ref[...], preferred_element_type=F32)
    logits = (both[:, :LANES] + both[:, LANES:]) + jnp.dot(lo, wr_ref[:, :LANES], preferred_element_type=F32)
    r_ref[0], cnt_ref[...] = _route(logits, tri_ref[...], cnt_ref[...])


def _merge(ya, yb, yc, g, x, gate, shift, scale, wa, wb, wc, wo, wr, base, *, tm, name):
    bx, t, d = x.shape
    row = lambda b, i: (b, 0, 0)
    tile = lambda b, i: (b, i, 0)
    ids = jnp.arange(tm, dtype=jnp.int32)
    tri = (ids[None, :] < ids[:, None]).astype(BF16)
    return pl.pallas_call(
        _merge_kernel,
        grid=(bx, t // tm),
        in_specs=[pl.BlockSpec((1, tm, 512), tile)] * 3
        + [pl.BlockSpec((1, tm, G_COLS), tile), pl.BlockSpec((1, tm, d), tile)]
        + [pl.BlockSpec((1, 1, d), row)] * 3
        + [_resident((512, d))] * 3 + [_resident((d, d)), _resident((d, 2 * LANES)), _resident((tm, tm)),
                                      _resident((1, LANES))],
        out_specs=[pl.BlockSpec((1, tm, d), tile), pl.BlockSpec((1, tm, d), tile), pl.BlockSpec((1, tm, LANES), tile),
                   pl.BlockSpec((1, LANES), lambda b, i: (0, 0))],
        out_shape=[jax.ShapeDtypeStruct((bx, t, d), F32), jax.ShapeDtypeStruct((bx, t, d), F32),
                   jax.ShapeDtypeStruct((bx, t, LANES), F32), jax.ShapeDtypeStruct((1, LANES), F32)],
        compiler_params=_params(2),
        name=name,
    )(ya, yb, yc, g, x, gate, shift, scale, wa, wb, wc, wo, wr, tri, base)


def _dispatch_kernel(dest_ref, h_ref, xs_in_ref, xs_ref, sem, *, tokens):
    del xs_in_ref

    def row_copy(r, d):
        return pltpu.make_async_copy(h_ref.at[pl.ds(r, 1), :], xs_ref.at[pl.ds(d, 1), :], sem)

    def issue(r, carry):
        for k in range(2):
            row_copy(r, dest_ref[0, 0, 2 * r + k]).start()
        return carry

    lax.fori_loop(0, tokens, issue, 0, unroll=8)
    for _ in range(2):
        pltpu.make_async_copy(h_ref, xs_ref.at[pl.ds(0, tokens), :], sem).wait()


def _dispatch(dest, h, xs, *, td):
    n, d = h.shape
    return pl.pallas_call(
        functools.partial(_dispatch_kernel, tokens=td),
        grid=(n // td,),
        in_specs=[pl.BlockSpec((1, 1, 2 * td), lambda i: (i, 0, 0), memory_space=pltpu.SMEM),
                  pl.BlockSpec((td, d), lambda i: (i, 0)),
                  pl.BlockSpec(memory_space=pl.ANY)],
        out_specs=pl.BlockSpec(memory_space=pl.ANY),
        out_shape=jax.ShapeDtypeStruct(xs.shape, xs.dtype),
        scratch_shapes=[pltpu.SemaphoreType.DMA(())],
        input_output_aliases={2: 0},
        compiler_params=_params(1, row_dma=True),
        name="moe_dispatch",
    )(dest.reshape(n // td, 1, 2 * td), h, xs)


def _experts_kernel(te_ref, used_ref, xs_ref, wg_ref, wu_ref, wd_ref, y_ref, wg_bf, wu_bf, wd_bf):
    t = pl.program_id(0)
    live = t < used_ref[0]

    @pl.when(live & ((t == 0) | (te_ref[t] != te_ref[jnp.maximum(t - 1, 0)])))
    def _():
        wg_bf[...] = wg_ref[0, 0].astype(BF16)
        wu_bf[...] = wu_ref[0, 0].astype(BF16)
        wd_bf[...] = wd_ref[0, 0].astype(BF16)

    @pl.when(live)
    def _():
        xb = xs_ref[...].astype(BF16)
        a = jnp.dot(xb, wg_bf[...], preferred_element_type=F32)
        u = jnp.dot(xb, wu_bf[...], preferred_element_type=F32)
        y_ref[...] = jnp.dot((a * _sigmoid(a) * u).astype(BF16), wd_bf[...], preferred_element_type=F32)

    @pl.when(t >= used_ref[0])
    def _():
        y_ref[...] = jnp.zeros_like(y_ref)


def _experts(tile_expert, n_used, xs, wg, wu, wd, *, layer, tm):
    p, d = xs.shape
    hid = wg.shape[3]
    weights = lambda t, te, nu: (layer, te[t], 0, 0)
    return pl.pallas_call(
        _experts_kernel,
        grid_spec=pltpu.PrefetchScalarGridSpec(
            num_scalar_prefetch=2,
            grid=(p // tm,),
            in_specs=[pl.BlockSpec((tm, d), lambda t, te, nu: (t, 0)),
                      pl.BlockSpec((1, 1, d, hid), weights), pl.BlockSpec((1, 1, d, hid), weights),
                      pl.BlockSpec((1, 1, hid, d), weights)],
            out_specs=pl.BlockSpec((tm, d), lambda t, te, nu: (t, 0)),
            scratch_shapes=[pltpu.VMEM((d, hid), BF16), pltpu.VMEM((d, hid), BF16), pltpu.VMEM((hid, d), BF16)]),
        out_shape=jax.ShapeDtypeStruct((p, d), F32),
        compiler_params=_params(1),
        name="moe_experts",
    )(tile_expert, n_used, xs, wg, wu, wd)


def _combine_kernel(dest_ref, y_ref, x_ref, gate_ref, r_ref, o_ref, buf, sem, *, tokens):
    def row_copy(r, k, d):
        return pltpu.make_async_copy(y_ref.at[pl.ds(d, 1), :], buf.at[k, pl.ds(r, 1), :], sem)

    def issue(r, carry):
        for k in range(2):
            row_copy(r, k, dest_ref[0, 0, 2 * r + k]).start()
        return carry

    lax.fori_loop(0, tokens, issue, 0, unroll=8)
    for k in range(2):
        pltpu.make_async_copy(y_ref.at[pl.ds(0, tokens), :], buf.at[k], sem).wait()
    w = r_ref[...]
    o_ref[0] = x_ref[0] + gate_ref[0] * (w[:, 2:3] * buf[0] + w[:, 3:4] * buf[1])


def _combine(dest, y, x, gate, route, *, tc, name):
    bx, t, d = x.shape
    per = t // tc
    return pl.pallas_call(
        functools.partial(_combine_kernel, tokens=tc),
        grid=(bx * per,),
        in_specs=[pl.BlockSpec((1, 1, 2 * tc), lambda i: (i, 0, 0), memory_space=pltpu.SMEM),
                  pl.BlockSpec(memory_space=pl.ANY),
                  pl.BlockSpec((1, tc, d), lambda i: (i // per, i % per, 0)),
                  pl.BlockSpec((1, 1, d), lambda i: (i // per, 0, 0)),
                  pl.BlockSpec((tc, LANES), lambda i: (i, 0))],
        out_specs=pl.BlockSpec((1, tc, d), lambda i: (i // per, i % per, 0)),
        out_shape=jax.ShapeDtypeStruct(x.shape, F32),
        scratch_shapes=[pltpu.VMEM((2, tc, d), F32), pltpu.SemaphoreType.DMA(())],
        compiler_params=_params(1, row_dma=True),
        name=name,
    )(dest.reshape(bx * per, 1, 2 * tc), y, x, gate, route)


def _expert_slots(routes, counts, tm):
    n_assign = 2 * sum(r.shape[0] for r in routes)
    padded = ((counts + tm - 1) // tm) * tm
    ends = jnp.cumsum(padded)
    starts = ends - padded
    experts = jnp.arange(N_EXPERTS, dtype=jnp.int32)
    dests = []
    for r in routes:
        e = r[:, 0:2].astype(jnp.int32)
        rank = r[:, 4:6].astype(jnp.int32)
        start = jnp.sum(jnp.where(e[:, :, None] == experts, starts, 0), axis=-1)
        dests.append((start + rank).reshape(-1))
    n_tiles = n_assign // tm + N_EXPERTS
    tiles = jnp.arange(n_tiles, dtype=jnp.int32)
    tile_expert = jnp.minimum(jnp.sum((tiles[:, None] >= (ends // tm)[None, :]).astype(jnp.int32), axis=1),
                              N_EXPERTS - 1)
    return dests, tile_expert, (ends[-1:] // tm).astype(jnp.int32), n_tiles


def _rope_tables(length):
    pairs = HEAD_DIM // 4
    pos = jnp.arange(length, dtype=jnp.int32)
    row = (pos // GRID_W).astype(F32)
    col = (pos % GRID_W).astype(F32)
    freqs = ROPE_THETA ** (-jnp.arange(pairs, dtype=F32) / pairs)
    ang = jnp.concatenate([row[:, None] * freqs, col[:, None] * freqs], axis=-1)
    cos = jnp.tile(jnp.cos(ang), (1, 4))
    sin = jnp.tile(jnp.concatenate([-jnp.sin(ang), jnp.sin(ang)], axis=-1), (1, 2))
    return cos, sin


def kernel(x, c, ctx, c_ctx, w_ada, b_ada, w_in, a_qnorm, a_knorm, b_qnorm, b_knorm, c_qnorm, c_knorm, lambda_q1, lambda_k1, lambda_q2, lambda_k2, b_subln, c_sink, w_branch_a, w_branch_b, w_branch_c, w_out, w_router_group, w_router_expert, w_exp_gate, w_exp_up, w_exp_down):
    bsz, length, d = x.shape
    c_len = ctx.shape[1]
    depth = w_ada.shape[0]
    assert d == D_MODEL and bsz + 1 <= MOD_ROWS and length % 512 == 0 and c_len == 256

    cos, sin = _rope_tables(length)
    no_rope = jnp.zeros((c_len, LANES), F32)
    head_ids = jnp.arange(NORM_CHUNK, dtype=jnp.int32) // HEAD_DIM
    ones = (head_ids[:, None] == head_ids[None, :]).astype(BF16)
    c_rows = jnp.concatenate([c, c_ctx[None, :], jnp.zeros((MOD_ROWS - bsz - 1, d), F32)], axis=0)
    mod_all = _ada(c_rows, w_ada, b_ada.reshape(depth, 1, -1))
    zero_sink = jnp.zeros((8,), F32)

    tq_a, tq_b = 256, 256
    tk = 2048 if length % 2048 == 0 else 512
    n_lat = length // tk
    tm_moe = 512

    for l in range(depth):
        need_ctx = l < depth - 1
        mod = lambda k: mod_all[l, :bsz, k * d:(k + 1) * d].reshape(bsz, 1, d)
        cmod = lambda k: jnp.broadcast_to(mod_all[l, bsz, k * d:(k + 1) * d], (bsz, 1, d))
        lam_init = 0.8 - 0.6 * math.exp(-0.3 * l)
        lam = (jnp.exp(jnp.sum(lambda_q1[l] * lambda_k1[l])) - jnp.exp(jnp.sum(lambda_q2[l] * lambda_k2[l]))
               + lam_init).reshape(1).astype(F32)

        w = w_in[l]
        cols = lambda name: w[:, W_IN_COLS[name][0]:W_IN_COLS[name][1]]
        wqk = jnp.concatenate([cols(n) for n in QK_ORDER], axis=1).astype(BF16)
        wg = cols("gates").astype(BF16)
        wv = jnp.concatenate([cols(n) for n in V_ORDER], axis=1).astype(BF16)
        q_scale = HEAD_DIM ** -0.5 * LOG2E
        norm_gain = {"a_q": a_qnorm[l] * q_scale, "b_q": b_qnorm[l] * q_scale, "c_q": c_qnorm[l] * q_scale,
                     "b_k": b_knorm[l], "a_k": a_knorm[l], "c_k": c_knorm[l]}
        gain = jnp.concatenate([jnp.tile(norm_gain[n], (W_IN_COLS[n][1] - W_IN_COLS[n][0]) // HEAD_DIM)
                                for n in QK_ORDER]).reshape(1, QK_COLS)
        subln = b_subln[l].reshape(1, LANES)
        sink = c_sink[l] * LOG2E
        w_router = jnp.concatenate([w_router_group[l], w_router_expert[l],
                                    jnp.zeros((d, LANES - N_GROUPS - N_EXPERTS), F32)], axis=1)
        w_router_hi = w_router.astype(BF16)
        w_router_lo = (w_router - w_router_hi.astype(F32)).astype(BF16)
        merge_w = (w_branch_a[l].astype(BF16), w_branch_b[l].astype(BF16), w_branch_c[l].astype(BF16),
                   w_out[l].astype(BF16), jnp.concatenate([w_router_hi, w_router_lo], axis=1))

        qk, gates, v = _inproj(x, mod(0), mod(1), wqk, wg, wv, gain, ones, cos, sin, use_rope=True, tm=512)
        cqk, cgates, cv = _inproj(ctx, cmod(0), cmod(1), wqk, wg, wv, gain, ones, no_rope, no_rope,
                                  use_rope=False, tm=c_len)
        ya = _gqa(zero_sink, qk, qk, v, cqk, cv, mixer="a", tq=tq_a, tk=tk, n_lat=n_lat,
                  use_sink=False, name="attn_gqa")
        yb = _diff(lam, qk, qk, v, cqk, cv, subln, tq=tq_b, tk=tk, n_lat=n_lat, out_scale=1.0 - lam_init,
                   name="attn_diff")
        yc = _window(sink, qk, qk, v, cqk, cv, tq=128, nb=4)
        x, h2, route, counts = _merge(ya, yb, yc, gates, x, mod(2), mod(3), mod(4), *merge_w,
                                      jnp.zeros((1, LANES), F32), tm=512, name="merge")
        tokens = [h2.reshape(bsz * length, d)]
        routes = [route.reshape(bsz * length, LANES)]
        if need_ctx:
            cya = _gqa(zero_sink, cqk, cqk, cv, cqk, cv, mixer="a", tq=128, tk=tk, n_lat=0,
                       use_sink=False, name="ctx_gqa")
            cyb = _diff(lam, cqk, cqk, cv, cqk, cv, subln, tq=c_len, tk=tk, n_lat=0, out_scale=1.0 - lam_init,
                        name="ctx_diff")
            cyc = _gqa(sink, cqk, cqk, cv, cqk, cv, mixer="c", tq=128, tk=tk, n_lat=0,
                       use_sink=True, name="ctx_sink")
            ctx, hc2, croute, counts = _merge(cya, cyb, cyc, cgates, ctx, cmod(2), cmod(3), cmod(4), *merge_w, counts,
                                              tm=c_len, name="ctx_merge")
            tokens.append(hc2.reshape(bsz * c_len, d))
            routes.append(croute.reshape(bsz * c_len, LANES))

        dests, tile_expert, n_used, n_tiles = _expert_slots(routes, counts[0, :N_EXPERTS].astype(jnp.int32), tm_moe)
        xs = jnp.zeros((n_tiles * tm_moe, d), F32)
        for t, dest in zip(tokens, dests):
            xs = _dispatch(dest, t, xs, td=512)
        y = _experts(tile_expert, n_used, xs, w_exp_gate, w_exp_up, w_exp_down, layer=l, tm=tm_moe)
        x = _combine(dests[0], y, x, mod(5), routes[0], tc=512, name="moe_combine")
        if need_ctx:
            ctx = _combine(dests[1], y, ctx, cmod(5), routes[1], tc=c_len, name="ctx_combine")
    return x
```

```python
import functools
import math

import jax
import jax.numpy as jnp
from jax import lax
from jax.experimental import pallas as pl
from jax.experimental.pallas import tpu as pltpu

F32 = jnp.float32
BF16 = jnp.bfloat16

D_MODEL = 1024
HEAD_DIM = 64
GRID_W = 64
ROPE_THETA = 10000.0
WINDOW = 128
N_GROUPS = 4
EXPERTS_PER_GROUP = 8
N_EXPERTS = N_GROUPS * EXPERTS_PER_GROUP
EXPERT_HIDDEN = D_MODEL // 2
N_MOD = 6
EPS = 1e-6
NEG = -1e30
LOG2E = math.log2(math.e)
LANES = 128
MOD_ROWS = 16

QK_COLS = 2304
G_COLS = 3 * D_MODEL
V_COLS = 768
MIXER_WIDTH = 512
W_IN_COLS = {"a_q": (0, 512), "b_q": (512, 1024), "c_q": (1024, 1536), "gates": (1536, 4608), "a_k": (4608, 4736),
             "a_v": (4736, 4864), "b_k": (4864, 5376), "b_v": (5376, 5888), "c_k": (5888, 6016), "c_v": (6016, 6144)}
QK_ORDER = ("a_q", "b_q", "c_q", "b_k", "a_k", "c_k")
V_ORDER = ("b_v", "a_v", "c_v")


def _offsets(order):
    out, at = {}, 0
    for name in order:
        out[name] = at
        at += W_IN_COLS[name][1] - W_IN_COLS[name][0]
    return out


QK_OFFSET = _offsets(QK_ORDER)
V_OFFSET = _offsets(V_ORDER)
NORM_CHUNK = 256
SUB = 256
GQA_STREAM_ROWS = 256

VMEM_LIMIT = 48 * 1024 * 1024


def _params(n_axes, vmem=VMEM_LIMIT, row_dma=False):
    return pltpu.CompilerParams(dimension_semantics=("arbitrary",) * n_axes, vmem_limit_bytes=vmem,
                                disable_bounds_checks=row_dma)


def _resident(shape):
    return pl.BlockSpec(shape, lambda *_: (0,) * len(shape), pipeline_mode=pl.Buffered(1))


def _sigmoid(v):
    return 1.0 / (1.0 + jnp.exp(-v))


def _rms_modulate(v, shift, scale):
    v = v * lax.rsqrt(jnp.mean(v * v, axis=-1, keepdims=True) + EPS)
    return v * (1.0 + scale) + shift


def _ada_kernel(c_ref, w_ref, b_ref, o_ref):
    c = c_ref[...]
    o_ref[0] = jnp.dot(c * _sigmoid(c), w_ref[0], preferred_element_type=F32,
                       precision=lax.Precision.HIGHEST) + b_ref[0]


def _ada(c_rows, w_ada, b_ada):
    depth, d, n = w_ada.shape
    tn = 1536
    return pl.pallas_call(
        _ada_kernel,
        grid=(depth, n // tn),
        in_specs=[pl.BlockSpec((MOD_ROWS, d), lambda l, j: (0, 0)),
                  pl.BlockSpec((1, d, tn), lambda l, j: (l, 0, j)),
                  pl.BlockSpec((1, 1, tn), lambda l, j: (l, 0, j))],
        out_specs=pl.BlockSpec((1, MOD_ROWS, tn), lambda l, j: (l, 0, j)),
        out_shape=jax.ShapeDtypeStruct((depth, MOD_ROWS, n), F32),
        compiler_params=_params(2),
        name="ada",
    )(c_rows, w_ada, b_ada)


def _rope_partner(z):
    lane = lax.broadcasted_iota(jnp.int32, z.shape, 1)
    return jnp.where((lane & 32) == 0, pltpu.roll(z, 96, 1), pltpu.roll(z, 32, 1))


def _inproj_kernel(x_ref, shift_ref, scale_ref, wqk_ref, wg_ref, wv_ref, gain_ref, ones_ref, cos_ref, sin_ref,
                   oqk_ref, og_ref, ov_ref, *, use_rope):
    hb = _rms_modulate(x_ref[0], shift_ref[0], scale_ref[0]).astype(BF16)
    chunks = [slice(c * NORM_CHUNK, (c + 1) * NORM_CHUNK) for c in range(QK_COLS // NORM_CHUNK)]
    ys = [jnp.dot(hb, wqk_ref[:, cols], preferred_element_type=F32) for cols in chunks]
    sss = [jnp.dot((y * y).astype(BF16), ones_ref[...], preferred_element_type=F32) for y in ys]
    for c in range(G_COLS // 512):
        cols = slice(c * 512, (c + 1) * 512)
        og_ref[0, :, cols] = _sigmoid(jnp.dot(hb, wg_ref[:, cols], preferred_element_type=F32)).astype(BF16)
    ov_ref[0] = jnp.dot(hb, wv_ref[...], preferred_element_type=F32).astype(BF16)
    for cols, y, ss in zip(chunks, ys, sss):
        yn = y * lax.rsqrt(ss * (1.0 / HEAD_DIM) + EPS) * gain_ref[:, cols]
        if use_rope:
            halves = []
            for k in range(NORM_CHUNK // LANES):
                z = yn[:, k * LANES:(k + 1) * LANES]
                halves.append(z * cos_ref[...] + _rope_partner(z) * sin_ref[...])
            yn = jnp.concatenate(halves, axis=1)
        oqk_ref[0, :, cols] = yn.astype(BF16)


def _inproj(x, shift, scale, wqk, wg, wv, gain, ones, cos, sin, *, use_rope, tm):
    bx, t, d = x.shape
    row = lambda b, i: (b, 0, 0)
    tile = lambda b, i: (b, i, 0)
    return pl.pallas_call(
        functools.partial(_inproj_kernel, use_rope=use_rope),
        grid=(bx, t // tm),
        in_specs=[pl.BlockSpec((1, tm, d), tile),
                  pl.BlockSpec((1, 1, d), row), pl.BlockSpec((1, 1, d), row),
                  _resident((d, QK_COLS)), _resident((d, G_COLS)), _resident((d, V_COLS)),
                  _resident((1, QK_COLS)), _resident((NORM_CHUNK, NORM_CHUNK)),
                  pl.BlockSpec((tm, LANES), lambda b, i: (i, 0)), pl.BlockSpec((tm, LANES), lambda b, i: (i, 0))],
        out_specs=[pl.BlockSpec((1, tm, QK_COLS), tile), pl.BlockSpec((1, tm, G_COLS), tile),
                   pl.BlockSpec((1, tm, V_COLS), tile)],
        out_shape=[jax.ShapeDtypeStruct((bx, t, QK_COLS), BF16), jax.ShapeDtypeStruct((bx, t, G_COLS), BF16),
                   jax.ShapeDtypeStruct((bx, t, V_COLS), BF16)],
        compiler_params=_params(2),
        name="inproj_rope" if use_rope else "inproj_ctx",
    )(x, shift, scale, wqk, wg, wv, gain, ones, cos, sin)


def _qk(qs, k):
    return lax.dot_general(qs, k, (((1,), (1,)), ((), ())), preferred_element_type=F32)


def _online_softmax(streams, state, kl_ref, vl_ref, kc_ref, vc_ref, n_lat, tk):
    n = len(streams)
    state = [state[4 * i:4 * i + 4] for i in range(n)]
    for (qs, _, m0, l0), (qs_ref, m_ref, l_ref, acc_ref) in zip(streams, state):
        lane = lax.broadcasted_iota(jnp.int32, m_ref.shape, 1)
        qs_ref[...] = qs
        m_ref[...] = jnp.broadcast_to(m0, m_ref.shape)
        l_ref[...] = jnp.where(lane == 0, l0, 0.0)
        acc_ref[...] = jnp.zeros(acc_ref.shape, F32)

    def scores(i, key_block, width):
        qs = state[i][0][...]
        return [_qk(qs, key_block(c)) for c in range(width // SUB)]

    def update(i, subs, value_block):
        _, m_ref, l_ref, acc_ref = state[i]
        m = m_ref[...]
        top = subs[0]
        for s in subs[1:]:
            top = jnp.maximum(top, s)
        top = functools.reduce(jnp.maximum, [top[:, k * LANES:(k + 1) * LANES] for k in range(SUB // LANES)])
        m_new = jnp.maximum(m, jnp.max(top, axis=-1, keepdims=True))
        alpha = jnp.exp2(m - m_new)
        m_wide = jnp.concatenate([m_new] * (SUB // LANES), axis=1)
        l = alpha * l_ref[...]
        acc = alpha * acc_ref[...]
        for c, s in enumerate(subs):
            p = jnp.exp2(s - m_wide)
            for k in range(SUB // LANES):
                l = l + p[:, k * LANES:(k + 1) * LANES]
            acc = acc + jnp.dot(p.astype(BF16), value_block(c), preferred_element_type=F32)
        m_ref[...] = m_new
        l_ref[...] = l
        acc_ref[...] = acc

    def step(key_block, value_block, width):
        subs = [scores(i, functools.partial(key_block, lanes), width) for i, (_, lanes, _, _) in enumerate(streams)]
        for i, (_, lanes, _, _) in enumerate(streams):
            update(i, subs[i], functools.partial(value_block, lanes))

    if n_lat:
        def body(j, carry):
            rows = lambda c: pl.ds(pl.multiple_of(j * tk + c * SUB, SUB), SUB)
            step(lambda lanes, c: kl_ref[0, rows(c), lanes], lambda lanes, c: vl_ref[0, rows(c), lanes], tk)
            return carry
        lax.fori_loop(0, n_lat, body, 0)
    step(lambda lanes, c: kc_ref[0, c * SUB:(c + 1) * SUB, lanes],
         lambda lanes, c: vc_ref[0, c * SUB:(c + 1) * SUB, lanes], kc_ref.shape[1])
    return [acc_ref[...] / jnp.sum(l_ref[...], axis=-1, keepdims=True) for _, _, l_ref, acc_ref in state]


def _softmax_state(n_streams, rows):
    return [pltpu.VMEM((rows, LANES), BF16), pltpu.VMEM((rows, LANES), F32), pltpu.VMEM((rows, LANES), F32),
            pltpu.VMEM((rows, LANES), F32)] * n_streams


def _stack_pair(q_ref, c, tq, row0=0):
    g = c // 2
    lane = lax.broadcasted_iota(jnp.int32, (tq, LANES), 1)
    keep = (lane >= HEAD_DIM) if g else (lane < HEAD_DIM)
    z = q_ref[0, row0:row0 + tq, c * LANES:(c + 1) * LANES].astype(F32)
    swapped = pltpu.roll(z, HEAD_DIM, 1)
    first, second = (z, swapped) if g == 0 else (swapped, z)
    return jnp.concatenate([jnp.where(keep, first, 0.0).astype(BF16), jnp.where(keep, second, 0.0).astype(BF16)], axis=0)


def _unstack_pair(o, c, tq):
    lane = lax.broadcasted_iota(jnp.int32, (tq, LANES), 1)
    a, b = o[:tq], o[tq:]
    if c // 2 == 0:
        b = pltpu.roll(b, HEAD_DIM, 1)
    else:
        a = pltpu.roll(a, HEAD_DIM, 1)
    return jnp.where(lane < HEAD_DIM, a, b)


def _sink_rows(sink_ref, heads, tq):
    return jnp.concatenate([jnp.full((tq, 1), sink_ref[h], F32) for h in heads], axis=0)


def _gqa_kernel(sink_ref, q_ref, kl_ref, vl_ref, kc_ref, vc_ref, o_ref, *state, tq, tk, n_lat, use_sink):
    rs = min(tq, GQA_STREAM_ROWS)
    per_pair = 2 * tq // rs
    streams = []
    for c in range(4):
        pair = _stack_pair(q_ref, c, tq)
        for i in range(per_pair):
            h = 2 * c + i * rs // tq
            if use_sink:
                m0, l0 = jnp.full((rs, 1), sink_ref[h], F32), jnp.ones((rs, 1), F32)
            else:
                m0, l0 = jnp.full((rs, 1), NEG, F32), jnp.zeros((rs, 1), F32)
            streams.append((pair[i * rs:(i + 1) * rs], slice(0, LANES), m0, l0))
    outs = _online_softmax(streams, state, kl_ref, vl_ref, kc_ref, vc_ref, n_lat, tk)
    for c in range(4):
        pair = jnp.concatenate(outs[c * per_pair:(c + 1) * per_pair], axis=0)
        o_ref[0, :, c * LANES:(c + 1) * LANES] = _unstack_pair(pair, c, tq).astype(BF16)


def _diff_kernel(lam_ref, q_ref, kl_ref, vl_ref, kc_ref, vc_ref, gain_ref, o_ref, *state, tq, tk, n_lat, out_scale):
    lane = lax.broadcasted_iota(jnp.int32, (tq, LANES), 1)
    streams = []
    for n in range(4):
        lanes = slice(n * LANES, (n + 1) * LANES)
        z = q_ref[0, :, lanes]
        zero = jnp.zeros_like(z)
        for qs in (jnp.where(lane < HEAD_DIM, z, zero), jnp.where(lane >= HEAD_DIM, z, zero)):
            streams.append((qs, lanes, jnp.full((tq, 1), NEG, F32), jnp.zeros((tq, 1), F32)))
    outs = _online_softmax(streams, state, kl_ref, vl_ref, kc_ref, vc_ref, n_lat, tk)
    for n in range(4):
        d = outs[2 * n] - lam_ref[0] * outs[2 * n + 1]
        d = d * lax.rsqrt(jnp.mean(d * d, axis=-1, keepdims=True) + EPS) * gain_ref[...] * out_scale
        o_ref[0, :, n * LANES:(n + 1) * LANES] = d.astype(BF16)


def _lane_blocks(a):
    return [a[:, k * LANES:(k + 1) * LANES] for k in range(a.shape[1] // LANES)]


def _window_kernel(sink_ref, q_ref, kl_ref, vl_ref, kc_ref, vc_ref, o_ref, *, tq, nb, length):
    span = tq + 2 * WINDOW
    kc = kc_ref[0]
    vc = vc_ref[0]
    rel = lax.broadcasted_iota(jnp.int32, (tq, span), 1) - lax.broadcasted_iota(jnp.int32, (tq, span), 0)
    chains = [(j, h) for j in range(nb) for h in range(8)]
    starts = []
    for j in range(nb):
        first = (pl.program_id(1) * nb + j) * tq
        starts.append((first, pl.multiple_of(jnp.clip(first - WINDOW, 0, length - span), LANES)))
    sw, sc = [], []
    for j in range(nb):
        first, start = starts[j]
        valid = jnp.abs(rel + (start - first)) <= WINDOW
        for c in range(4):
            pair = _stack_pair(q_ref, c, tq, j * tq)
            for qs in (pair[:tq], pair[tq:]):
                sw.append(jnp.where(valid, _qk(qs, kl_ref[0, pl.ds(start, span), :]), NEG))
                sc.append(_qk(qs, kc))
    snk = [jnp.full((tq, 1), sink_ref[h], F32) for _, h in chains]
    top = [jnp.max(functools.reduce(jnp.maximum, _lane_blocks(a) + _lane_blocks(b)), axis=-1, keepdims=True)
           for a, b in zip(sw, sc)]
    m = [jnp.maximum(t, s) for t, s in zip(top, snk)]
    pw = [jnp.exp2(a - mm) for a, mm in zip(sw, m)]
    pc = [jnp.exp2(b - mm) for b, mm in zip(sc, m)]
    tot = [jnp.sum(functools.reduce(jnp.add, _lane_blocks(a) + _lane_blocks(b)), axis=-1, keepdims=True)
           for a, b in zip(pw, pc)]
    l = [t + jnp.exp2(s - mm) for t, s, mm in zip(tot, snk, m)]
    o = [jnp.dot(a.astype(BF16), vl_ref[0, pl.ds(starts[j][1], span), :], preferred_element_type=F32)
         + jnp.dot(b.astype(BF16), vc, preferred_element_type=F32) for (j, _), a, b in zip(chains, pw, pc)]
    o = [a / ll for a, ll in zip(o, l)]
    for j in range(nb):
        for c in range(4):
            pair = jnp.concatenate([o[8 * j + 2 * c], o[8 * j + 2 * c + 1]], axis=0)
            o_ref[0, j * tq:(j + 1) * tq, c * LANES:(c + 1) * LANES] = _unstack_pair(pair, c, tq).astype(BF16)


_SMEM = pl.BlockSpec(memory_space=pltpu.SMEM)


def _q_spec(mixer, rows):
    blk = QK_OFFSET[mixer + "_q"] // MIXER_WIDTH
    return pl.BlockSpec((1, rows, MIXER_WIDTH), lambda b, i: (b, i, blk))


def _kv_specs(klat, kctx, mixer, width):
    s_lat, s_ctx = klat.shape[1], kctx.shape[1]
    k_blk, v_blk = QK_OFFSET[mixer + "_k"] // width, V_OFFSET[mixer + "_v"] // width
    return [pl.BlockSpec((1, s_lat, width), lambda b, i: (b, 0, k_blk)),
            pl.BlockSpec((1, s_lat, width), lambda b, i: (b, 0, v_blk)),
            pl.BlockSpec((1, s_ctx, width), lambda b, i: (b, 0, k_blk)),
            pl.BlockSpec((1, s_ctx, width), lambda b, i: (b, 0, v_blk))]


def _gqa(sink, q, klat, vlat, kctx, vctx, *, mixer, tq, tk, n_lat, use_sink, name):
    bx, t, _ = q.shape
    return pl.pallas_call(
        functools.partial(_gqa_kernel, tq=tq, tk=tk, n_lat=n_lat, use_sink=use_sink),
        grid=(bx, t // tq),
        in_specs=[_SMEM, _q_spec(mixer, tq)] + _kv_specs(klat, kctx, mixer, LANES),
        out_specs=pl.BlockSpec((1, tq, MIXER_WIDTH), lambda b, i: (b, i, 0)),
        out_shape=jax.ShapeDtypeStruct((bx, t, MIXER_WIDTH), BF16),
        scratch_shapes=_softmax_state(8 * tq // min(tq, GQA_STREAM_ROWS), min(tq, GQA_STREAM_ROWS)),
        compiler_params=_params(2),
        name=name,
    )(sink, q, klat, vlat, kctx, vctx)


def _diff(lam, q, klat, vlat, kctx, vctx, gain, *, tq, tk, n_lat, out_scale, name):
    bx, t, _ = q.shape
    return pl.pallas_call(
        functools.partial(_diff_kernel, tq=tq, tk=tk, n_lat=n_lat, out_scale=out_scale),
        grid=(bx, t // tq),
        in_specs=[_SMEM, _q_spec("b", tq)] + _kv_specs(klat, kctx, "b", MIXER_WIDTH)
        + [pl.BlockSpec((1, LANES), lambda b, i: (0, 0))],
        out_specs=pl.BlockSpec((1, tq, MIXER_WIDTH), lambda b, i: (b, i, 0)),
        out_shape=jax.ShapeDtypeStruct((bx, t, MIXER_WIDTH), BF16),
        scratch_shapes=_softmax_state(8, tq),
        compiler_params=_params(2),
        name=name,
    )(lam, q, klat, vlat, kctx, vctx, gain)


def _window(sink, q, klat, vlat, kctx, vctx, *, tq, nb):
    bx, t, _ = q.shape
    return pl.pallas_call(
        functools.partial(_window_kernel, tq=tq, nb=nb, length=t),
        grid=(bx, t // (nb * tq)),
        in_specs=[_SMEM, _q_spec("c", nb * tq)] + _kv_specs(klat, kctx, "c", LANES),
        out_specs=pl.BlockSpec((1, nb * tq, MIXER_WIDTH), lambda b, i: (b, i, 0)),
        out_shape=jax.ShapeDtypeStruct((bx, t, MIXER_WIDTH), BF16),
        compiler_params=_params(2),
        name="attn_window",
    )(sink, q, klat, vlat, kctx, vctx)


def _route(logits, tri, base):
    lane = lax.broadcasted_iota(jnp.int32, logits.shape, 1).astype(F32)
    first = lambda hit: jnp.min(jnp.where(hit, lane, float(LANES)), axis=-1, keepdims=True)
    gl = jnp.where(lane < N_GROUPS, logits, NEG)
    gmax = jnp.max(gl, axis=-1, keepdims=True)
    gidx = first(gl == gmax)
    gw = 1.0 / jnp.sum(jnp.exp(gl - gmax), axis=-1, keepdims=True)
    lo = N_GROUPS + gidx * EXPERTS_PER_GROUP
    el = jnp.where((lane >= lo) & (lane < lo + EXPERTS_PER_GROUP), logits, NEG)
    v1 = jnp.max(el, axis=-1, keepdims=True)
    i1 = first(el == v1)
    el = jnp.where(lane == i1, NEG, el)
    v2 = jnp.max(el, axis=-1, keepdims=True)
    i2 = first(el == v2)
    e = jnp.exp(v2 - v1)
    w1 = gw / (1.0 + e)
    w2 = gw * e / (1.0 + e)
    e1 = i1 - N_GROUPS
    e2 = i2 - N_GROUPS
    hit1 = jnp.where(lane == e1, 1.0, 0.0)
    hit2 = jnp.where(lane == e2, 1.0, 0.0)
    before1 = jnp.dot(tri, hit1.astype(BF16), preferred_element_type=F32) + base
    base = base + jnp.sum(hit1, axis=0, keepdims=True)
    before2 = jnp.dot(tri, hit2.astype(BF16), preferred_element_type=F32) + base
    base = base + jnp.sum(hit2, axis=0, keepdims=True)
    r1 = jnp.sum(hit1 * before1, axis=-1, keepdims=True)
    r2 = jnp.sum(hit2 * before2, axis=-1, keepdims=True)
    out = jnp.zeros_like(logits)
    for k, val in enumerate((e1, e2, w1, w2, r1, r2)):
        out = jnp.where(lane == k, val, out)
    return out, base


def _merge_kernel(ya_ref, yb_ref, yc_ref, g_ref, x_ref, gate_ref, shift_ref, scale_ref,
                  wa_ref, wb_ref, wc_ref, wo_ref, wr_ref, tri_ref, base_ref, xo_ref, h_ref, r_ref, cnt_ref):
    d = D_MODEL

    @pl.when((pl.program_id(0) == 0) & (pl.program_id(1) == 0))
    def _():
        cnt_ref[...] = base_ref[...]

    m = None
    for k, (y_ref, w_ref) in enumerate(((ya_ref, wa_ref), (yb_ref, wb_ref), (yc_ref, wc_ref))):
        t = g_ref[0, :, k * d:(k + 1) * d].astype(F32) * jnp.dot(y_ref[0], w_ref[...], preferred_element_type=F32)
        m = t if m is None else m + t
    xn = x_ref[0] + gate_ref[0] * jnp.dot(m.astype(BF16), wo_ref[...], preferred_element_type=F32)
    xo_ref[0] = xn
    h2 = _rms_modulate(xn, shift_ref[0], scale_ref[0])
    h_ref[0] = h2
    hi = h2.astype(BF16)
    lo = (h2 - hi.astype(F32)).astype(BF16)
    both = jnp.dot(hi, wr_ref[...], preferred_element_type=F32)
    logits = (both[:, :LANES] + both[:, LANES:]) + jnp.dot(lo, wr_ref[:, :LANES], preferred_element_type=F32)
    r_ref[0], cnt_ref[...] = _route(logits, tri_ref[...], cnt_ref[...])


def _merge(ya, yb, yc, g, x, gate, shift, scale, wa, wb, wc, wo, wr, base, *, tm, name):
    bx, t, d = x.shape
    row = lambda b, i: (b, 0, 0)
    tile = lambda b, i: (b, i, 0)
    ids = jnp.arange(tm, dtype=jnp.int32)
    tri = (ids[None, :] < ids[:, None]).astype(BF16)
    return pl.pallas_call(
        _merge_kernel,
        grid=(bx, t // tm),
        in_specs=[pl.BlockSpec((1, tm, 512), tile)] * 3
        + [pl.BlockSpec((1, tm, G_COLS), tile), pl.BlockSpec((1, tm, d), tile)]
        + [pl.BlockSpec((1, 1, d), row)] * 3
        + [_resident((512, d))] * 3 + [_resident((d, d)), _resident((d, 2 * LANES)), _resident((tm, tm)),
                                      _resident((1, LANES))],
        out_specs=[pl.BlockSpec((1, tm, d), tile), pl.BlockSpec((1, tm, d), tile), pl.BlockSpec((1, tm, LANES), tile),
                   pl.BlockSpec((1, LANES), lambda b, i: (0, 0))],
        out_shape=[jax.ShapeDtypeStruct((bx, t, d), F32), jax.ShapeDtypeStruct((bx, t, d), F32),
                   jax.ShapeDtypeStruct((bx, t, LANES), F32), jax.ShapeDtypeStruct((1, LANES), F32)],
        compiler_params=_params(2),
        name=name,
    )(ya, yb, yc, g, x, gate, shift, scale, wa, wb, wc, wo, wr, tri, base)


def _dispatch_kernel(fill_ref, dest_ref, ha_ref, hb_ref, xs_ref, zeros, sem, zero_sem, *, tokens, a_steps, tm):
    i = pl.program_id(0)

    def fill(k):
        return pltpu.make_async_copy(zeros, xs_ref.at[pl.ds(pl.multiple_of(fill_ref[k], tm), tm), :], zero_sem)

    @pl.when(i == 0)
    def _():
        zeros[...] = jnp.zeros(zeros.shape, F32)
        for k in range(fill_ref.shape[0]):
            @pl.when(fill_ref[k] >= 0)
            def _():
                fill(k).start()
        for k in range(fill_ref.shape[0]):
            @pl.when(fill_ref[k] >= 0)
            def _():
                fill(k).wait()

    def scatter(h_ref):
        def issue(r, carry):
            for k in range(2):
                pltpu.make_async_copy(h_ref.at[pl.ds(r, 1), :], xs_ref.at[pl.ds(dest_ref[0, 0, 2 * r + k], 1), :],
                                      sem).start()
            return carry

        lax.fori_loop(0, tokens, issue, 0, unroll=8)
        for _ in range(2):
            pltpu.make_async_copy(h_ref, xs_ref.at[pl.ds(0, tokens), :], sem).wait()

    @pl.when(i < a_steps)
    def _():
        scatter(ha_ref)

    @pl.when(i >= a_steps)
    def _():
        scatter(hb_ref)


def _dispatch(fill, dests, tokens, n_slots, *, td, tm):
    ha, hb = tokens[0], tokens[-1]
    d = ha.shape[1]
    a_steps = ha.shape[0] // td
    steps = a_steps + (hb.shape[0] // td if len(tokens) > 1 else 0)
    dest = dests[0] if len(dests) == 1 else jnp.concatenate(dests)
    return pl.pallas_call(
        functools.partial(_dispatch_kernel, tokens=td, a_steps=a_steps, tm=tm),
        grid_spec=pltpu.PrefetchScalarGridSpec(
            num_scalar_prefetch=1,
            grid=(steps,),
            in_specs=[pl.BlockSpec((1, 1, 2 * td), lambda i, f: (i, 0, 0), memory_space=pltpu.SMEM),
                      pl.BlockSpec((td, d), lambda i, f: (jnp.minimum(i, a_steps - 1), 0)),
                      pl.BlockSpec((td, d), lambda i, f: (jnp.maximum(i - a_steps, 0), 0))],
            out_specs=pl.BlockSpec(memory_space=pl.ANY),
            scratch_shapes=[pltpu.VMEM((tm, d), F32), pltpu.SemaphoreType.DMA(()), pltpu.SemaphoreType.DMA(())]),
        out_shape=jax.ShapeDtypeStruct((n_slots, d), F32),
        compiler_params=_params(1, row_dma=True),
        name="moe_dispatch",
    )(fill, dest.reshape(steps, 1, 2 * td), ha, hb)


def _experts_kernel(te_ref, used_ref, xs_ref, wg_ref, wu_ref, wd_ref, y_ref, wg_bf, wu_bf, wd_bf):
    t = pl.program_id(0)
    live = t < used_ref[0]

    @pl.when(live & ((t == 0) | (te_ref[t] != te_ref[jnp.maximum(t - 1, 0)])))
    def _():
        wg_bf[...] = wg_ref[0, 0].astype(BF16)
        wu_bf[...] = wu_ref[0, 0].astype(BF16)
        wd_bf[...] = wd_ref[0, 0].astype(BF16)

    @pl.when(live)
    def _():
        xb = xs_ref[...].astype(BF16)
        a = jnp.dot(xb, wg_bf[...], preferred_element_type=F32)
        u = jnp.dot(xb, wu_bf[...], preferred_element_type=F32)
        y_ref[...] = jnp.dot((a * _sigmoid(a) * u).astype(BF16), wd_bf[...], preferred_element_type=F32)

    @pl.when(t >= used_ref[0])
    def _():
        y_ref[...] = jnp.zeros_like(y_ref)


def _experts(tile_expert, n_used, xs, wg, wu, wd, *, layer, tm):
    p, d = xs.shape
    hid = wg.shape[3]
    weights = lambda t, te, nu: (layer, te[t], 0, 0)
    return pl.pallas_call(
        _experts_kernel,
        grid_spec=pltpu.PrefetchScalarGridSpec(
            num_scalar_prefetch=2,
            grid=(p // tm,),
            in_specs=[pl.BlockSpec((tm, d), lambda t, te, nu: (t, 0)),
                      pl.BlockSpec((1, 1, d, hid), weights), pl.BlockSpec((1, 1, d, hid), weights),
                      pl.BlockSpec((1, 1, hid, d), weights)],
            out_specs=pl.BlockSpec((tm, d), lambda t, te, nu: (t, 0)),
            scratch_shapes=[pltpu.VMEM((d, hid), BF16), pltpu.VMEM((d, hid), BF16), pltpu.VMEM((hid, d), BF16)]),
        out_shape=jax.ShapeDtypeStruct((p, d), F32),
        compiler_params=_params(1),
        name="moe_experts",
    )(tile_expert, n_used, xs, wg, wu, wd)


def _combine_kernel(dest_ref, y_ref, x_ref, gate_ref, r_ref, o_ref, buf, sem, *, tokens):
    def row_copy(r, k, d):
        return pltpu.make_async_copy(y_ref.at[pl.ds(d, 1), :], buf.at[k, pl.ds(r, 1), :], sem)

    def issue(r, carry):
        for k in range(2):
            row_copy(r, k, dest_ref[0, 0, 2 * r + k]).start()
        return carry

    lax.fori_loop(0, tokens, issue, 0, unroll=8)
    for k in range(2):
        pltpu.make_async_copy(y_ref.at[pl.ds(0, tokens), :], buf.at[k], sem).wait()
    w = r_ref[...]
    o_ref[0] = x_ref[0] + gate_ref[0] * (w[:, 2:3] * buf[0] + w[:, 3:4] * buf[1])


def _combine(dest, y, x, gate, route, *, tc, name):
    bx, t, d = x.shape
    per = t // tc
    return pl.pallas_call(
        functools.partial(_combine_kernel, tokens=tc),
        grid=(bx * per,),
        in_specs=[pl.BlockSpec((1, 1, 2 * tc), lambda i: (i, 0, 0), memory_space=pltpu.SMEM),
                  pl.BlockSpec(memory_space=pl.ANY),
                  pl.BlockSpec((1, tc, d), lambda i: (i // per, i % per, 0)),
                  pl.BlockSpec((1, 1, d), lambda i: (i // per, 0, 0)),
                  pl.BlockSpec((tc, LANES), lambda i: (i, 0))],
        out_specs=pl.BlockSpec((1, tc, d), lambda i: (i // per, i % per, 0)),
        out_shape=jax.ShapeDtypeStruct(x.shape, F32),
        scratch_shapes=[pltpu.VMEM((2, tc, d), F32), pltpu.SemaphoreType.DMA(())],
        compiler_params=_params(1, row_dma=True),
        name=name,
    )(dest.reshape(bx * per, 1, 2 * tc), y, x, gate, route)


def _expert_slots(routes, counts, tm):
    n_assign = 2 * sum(r.shape[0] for r in routes)
    padded = ((counts + tm - 1) // tm) * tm
    ends = jnp.cumsum(padded)
    starts = ends - padded
    experts = jnp.arange(N_EXPERTS, dtype=jnp.int32)
    dests = []
    for r in routes:
        e = r[:, 0:2].astype(jnp.int32)
        rank = r[:, 4:6].astype(jnp.int32)
        start = jnp.sum(jnp.where(e[:, :, None] == experts, starts, 0), axis=-1)
        dests.append((start + rank).reshape(-1))
    n_tiles = n_assign // tm + N_EXPERTS
    tiles = jnp.arange(n_tiles, dtype=jnp.int32)
    tile_expert = jnp.minimum(jnp.sum((tiles[:, None] >= (ends // tm)[None, :]).astype(jnp.int32), axis=1),
                              N_EXPERTS - 1)
    n_used = (ends[-1:] // tm).astype(jnp.int32)
    unused = n_used + jnp.arange(N_EXPERTS, dtype=jnp.int32)
    fill = jnp.concatenate([jnp.where(padded > 0, ends - tm, -1), jnp.where(unused < n_tiles, unused * tm, -1)])
    return dests, tile_expert, n_used, n_tiles, fill.astype(jnp.int32)


def _rope_tables(length):
    pairs = HEAD_DIM // 4
    pos = jnp.arange(length, dtype=jnp.int32)
    row = (pos // GRID_W).astype(F32)
    col = (pos % GRID_W).astype(F32)
    freqs = ROPE_THETA ** (-jnp.arange(pairs, dtype=F32) / pairs)
    ang = jnp.concatenate([row[:, None] * freqs, col[:, None] * freqs], axis=-1)
    cos = jnp.tile(jnp.cos(ang), (1, 4))
    sin = jnp.tile(jnp.concatenate([-jnp.sin(ang), jnp.sin(ang)], axis=-1), (1, 2))
    return cos, sin


def kernel(x, c, ctx, c_ctx, w_ada, b_ada, w_in, a_qnorm, a_knorm, b_qnorm, b_knorm, c_qnorm, c_knorm, lambda_q1, lambda_k1, lambda_q2, lambda_k2, b_subln, c_sink, w_branch_a, w_branch_b, w_branch_c, w_out, w_router_group, w_router_expert, w_exp_gate, w_exp_up, w_exp_down):
    bsz, length, d = x.shape
    c_len = ctx.shape[1]
    depth = w_ada.shape[0]
    assert d == D_MODEL and bsz + 1 <= MOD_ROWS and length % 512 == 0 and c_len == 256

    cos, sin = _rope_tables(length)
    no_rope = jnp.zeros((c_len, LANES), F32)
    head_ids = jnp.arange(NORM_CHUNK, dtype=jnp.int32) // HEAD_DIM
    ones = (head_ids[:, None] == head_ids[None, :]).astype(BF16)
    c_rows = jnp.concatenate([c, c_ctx[None, :], jnp.zeros((MOD_ROWS - bsz - 1, d), F32)], axis=0)
    mod_all = _ada(c_rows, w_ada, b_ada.reshape(depth, 1, -1))
    zero_sink = jnp.zeros((8,), F32)

    tq_a, tq_b = 256, 256
    tk = 2048 if length % 2048 == 0 else 512
    n_lat = length // tk
    tm_moe = 512

    for l in range(depth):
        need_ctx = l < depth - 1
        mod = lambda k: mod_all[l, :bsz, k * d:(k + 1) * d].reshape(bsz, 1, d)
        cmod = lambda k: jnp.broadcast_to(mod_all[l, bsz, k * d:(k + 1) * d], (bsz, 1, d))
        lam_init = 0.8 - 0.6 * math.exp(-0.3 * l)
        lam = (jnp.exp(jnp.sum(lambda_q1[l] * lambda_k1[l])) - jnp.exp(jnp.sum(lambda_q2[l] * lambda_k2[l]))
               + lam_init).reshape(1).astype(F32)

        w = w_in[l]
        cols = lambda name: w[:, W_IN_COLS[name][0]:W_IN_COLS[name][1]]
        wqk = jnp.concatenate([cols(n) for n in QK_ORDER], axis=1).astype(BF16)
        wg = cols("gates").astype(BF16)
        wv = jnp.concatenate([cols(n) for n in V_ORDER], axis=1).astype(BF16)
        q_scale = HEAD_DIM ** -0.5 * LOG2E
        norm_gain = {"a_q": a_qnorm[l] * q_scale, "b_q": b_qnorm[l] * q_scale, "c_q": c_qnorm[l] * q_scale,
                     "b_k": b_knorm[l], "a_k": a_knorm[l], "c_k": c_knorm[l]}
        gain = jnp.concatenate([jnp.tile(norm_gain[n], (W_IN_COLS[n][1] - W_IN_COLS[n][0]) // HEAD_DIM)
                                for n in QK_ORDER]).reshape(1, QK_COLS)
        subln = b_subln[l].reshape(1, LANES)
        sink = c_sink[l] * LOG2E
        w_router = jnp.concatenate([w_router_group[l], w_router_expert[l],
                                    jnp.zeros((d, LANES - N_GROUPS - N_EXPERTS), F32)], axis=1)
        w_router_hi = w_router.astype(BF16)
        w_router_lo = (w_router - w_router_hi.astype(F32)).astype(BF16)
        merge_w = (w_branch_a[l].astype(BF16), w_branch_b[l].astype(BF16), w_branch_c[l].astype(BF16),
                   w_out[l].astype(BF16), jnp.concatenate([w_router_hi, w_router_lo], axis=1))

        qk, gates, v = _inproj(x, mod(0), mod(1), wqk, wg, wv, gain, ones, cos, sin, use_rope=True, tm=512)
        cqk, cgates, cv = _inproj(ctx, cmod(0), cmod(1), wqk, wg, wv, gain, ones, no_rope, no_rope,
                                  use_rope=False, tm=c_len)
        ya = _gqa(zero_sink, qk, qk, v, cqk, cv, mixer="a", tq=tq_a, tk=tk, n_lat=n_lat,
                  use_sink=False, name="attn_gqa")
        yb = _diff(lam, qk, qk, v, cqk, cv, subln, tq=tq_b, tk=tk, n_lat=n_lat, out_scale=1.0 - lam_init,
                   name="attn_diff")
        yc = _window(sink, qk, qk, v, cqk, cv, tq=128, nb=4)
        x, h2, route, counts = _merge(ya, yb, yc, gates, x, mod(2), mod(3), mod(4), *merge_w,
                                      jnp.zeros((1, LANES), F32), tm=512, name="merge")
        tokens = [h2.reshape(bsz * length, d)]
        routes = [route.reshape(bsz * length, LANES)]
        if need_ctx:
            cya = _gqa(zero_sink, cqk, cqk, cv, cqk, cv, mixer="a", tq=128, tk=tk, n_lat=0,
                       use_sink=False, name="ctx_gqa")
            cyb = _diff(lam, cqk, cqk, cv, cqk, cv, subln, tq=c_len, tk=tk, n_lat=0, out_scale=1.0 - lam_init,
                        name="ctx_diff")
            cyc = _gqa(sink, cqk, cqk, cv, cqk, cv, mixer="c", tq=128, tk=tk, n_lat=0,
                       use_sink=True, name="ctx_sink")
            ctx, hc2, croute, counts = _merge(cya, cyb, cyc, cgates, ctx, cmod(2), cmod(3), cmod(4), *merge_w, counts,
                                              tm=c_len, name="ctx_merge")
            tokens.append(hc2.reshape(bsz * c_len, d))
            routes.append(croute.reshape(bsz * c_len, LANES))

        dests, tile_expert, n_used, n_tiles, fill = _expert_slots(routes, counts[0, :N_EXPERTS].astype(jnp.int32),
                                                                  tm_moe)
        xs = _dispatch(fill, dests, tokens, n_tiles * tm_moe, td=512, tm=tm_moe)
        y = _experts(tile_expert, n_used, xs, w_exp_gate, w_exp_up, w_exp_down, layer=l, tm=tm_moe)
        x = _combine(dests[0], y, x, mod(5), routes[0], tc=512, name="moe_combine")
        if need_ctx:
            ctx = _combine(dests[1], y, ctx, cmod(5), routes[1], tc=c_len, name="ctx_combine")
    return x
```

```python
import functools
import math

import jax
import jax.numpy as jnp
from jax import lax
from jax.experimental import pallas as pl
from jax.experimental.pallas import tpu as pltpu

F32 = jnp.float32
BF16 = jnp.bfloat16

D_MODEL = 1024
HEAD_DIM = 64
GRID_W = 64
ROPE_THETA = 10000.0
WINDOW = 128
N_GROUPS = 4
EXPERTS_PER_GROUP = 8
N_EXPERTS = N_GROUPS * EXPERTS_PER_GROUP
EPS = 1e-6
NEG = -1e30
LOG2E = math.log2(math.e)
LANES = 128
MOD_ROWS = 16

QK_COLS = 2304
G_COLS = 3 * D_MODEL
V_COLS = 768
MIXER_WIDTH = 512
W_IN_COLS = {"a_q": (0, 512), "b_q": (512, 1024), "c_q": (1024, 1536), "gates": (1536, 4608), "a_k": (4608, 4736),
             "a_v": (4736, 4864), "b_k": (4864, 5376), "b_v": (5376, 5888), "c_k": (5888, 6016), "c_v": (6016, 6144)}
QK_ORDER = ("a_q", "b_q", "c_q", "b_k", "a_k", "c_k")
V_ORDER = ("b_v", "a_v", "c_v")


def _offsets(order):
    out, at = {}, 0
    for name in order:
        out[name] = at
        at += W_IN_COLS[name][1] - W_IN_COLS[name][0]
    return out


QK_OFFSET = _offsets(QK_ORDER)
V_OFFSET = _offsets(V_ORDER)
NORM_CHUNK = 256
SUB = 256
GQA_STREAM_ROWS = 256

VMEM_LIMIT = 48 * 1024 * 1024


def _params(n_axes, vmem=VMEM_LIMIT, row_dma=False):
    return pltpu.CompilerParams(dimension_semantics=("arbitrary",) * n_axes, vmem_limit_bytes=vmem,
                                disable_bounds_checks=row_dma)


def _resident(shape):
    return pl.BlockSpec(shape, lambda *_: (0,) * len(shape), pipeline_mode=pl.Buffered(1))


def _sigmoid(v):
    return 1.0 / (1.0 + jnp.exp(-v))


def _rms_modulate(v, shift, scale):
    v = v * lax.rsqrt(jnp.mean(v * v, axis=-1, keepdims=True) + EPS)
    return v * (1.0 + scale) + shift


def _ada_kernel(c_ref, w_ref, b_ref, o_ref):
    c = c_ref[...]
    o_ref[0] = jnp.dot(c * _sigmoid(c), w_ref[0], preferred_element_type=F32,
                       precision=lax.Precision.HIGHEST) + b_ref[0]


def _ada(c_rows, w_ada, b_ada):
    depth, d, n = w_ada.shape
    tn = 1536
    return pl.pallas_call(
        _ada_kernel,
        grid=(depth, n // tn),
        in_specs=[pl.BlockSpec((MOD_ROWS, d), lambda l, j: (0, 0)),
                  pl.BlockSpec((1, d, tn), lambda l, j: (l, 0, j)),
                  pl.BlockSpec((1, 1, tn), lambda l, j: (l, 0, j))],
        out_specs=pl.BlockSpec((1, MOD_ROWS, tn), lambda l, j: (l, 0, j)),
        out_shape=jax.ShapeDtypeStruct((depth, MOD_ROWS, n), F32),
        compiler_params=_params(2),
        name="ada",
    )(c_rows, w_ada, b_ada)


def _rope_partner(z):
    lane = lax.broadcasted_iota(jnp.int32, z.shape, 1)
    return jnp.where((lane & 32) == 0, pltpu.roll(z, 96, 1), pltpu.roll(z, 32, 1))


def _inproj_kernel(x_ref, shift_ref, scale_ref, wqk_ref, wg_ref, wv_ref, gain_ref, ones_ref, cos_ref, sin_ref,
                   oqk_ref, og_ref, ov_ref, *, use_rope):
    hb = _rms_modulate(x_ref[0], shift_ref[0], scale_ref[0]).astype(BF16)
    chunks = [slice(c * NORM_CHUNK, (c + 1) * NORM_CHUNK) for c in range(QK_COLS // NORM_CHUNK)]
    ys = [jnp.dot(hb, wqk_ref[:, cols], preferred_element_type=F32) for cols in chunks]
    sss = [jnp.dot((y * y).astype(BF16), ones_ref[...], preferred_element_type=F32) for y in ys]
    for c in range(G_COLS // 512):
        cols = slice(c * 512, (c + 1) * 512)
        og_ref[0, :, cols] = _sigmoid(jnp.dot(hb, wg_ref[:, cols], preferred_element_type=F32)).astype(BF16)
    ov_ref[0] = jnp.dot(hb, wv_ref[...], preferred_element_type=F32).astype(BF16)
    for cols, y, ss in zip(chunks, ys, sss):
        yn = y * lax.rsqrt(ss * (1.0 / HEAD_DIM) + EPS) * gain_ref[:, cols]
        if use_rope:
            halves = []
            for k in range(NORM_CHUNK // LANES):
                z = yn[:, k * LANES:(k + 1) * LANES]
                halves.append(z * cos_ref[...] + _rope_partner(z) * sin_ref[...])
            yn = jnp.concatenate(halves, axis=1)
        oqk_ref[0, :, cols] = yn.astype(BF16)


def _inproj(x, shift, scale, wqk, wg, wv, gain, ones, cos, sin, *, use_rope, tm):
    bx, t, d = x.shape
    row = lambda b, i: (b, 0, 0)
    tile = lambda b, i: (b, i, 0)
    return pl.pallas_call(
        functools.partial(_inproj_kernel, use_rope=use_rope),
        grid=(bx, t // tm),
        in_specs=[pl.BlockSpec((1, tm, d), tile),
                  pl.BlockSpec((1, 1, d), row), pl.BlockSpec((1, 1, d), row),
                  _resident((d, QK_COLS)), _resident((d, G_COLS)), _resident((d, V_COLS)),
                  _resident((1, QK_COLS)), _resident((NORM_CHUNK, NORM_CHUNK)),
                  pl.BlockSpec((tm, LANES), lambda b, i: (i, 0)), pl.BlockSpec((tm, LANES), lambda b, i: (i, 0))],
        out_specs=[pl.BlockSpec((1, tm, QK_COLS), tile), pl.BlockSpec((1, tm, G_COLS), tile),
                   pl.BlockSpec((1, tm, V_COLS), tile)],
        out_shape=[jax.ShapeDtypeStruct((bx, t, QK_COLS), BF16), jax.ShapeDtypeStruct((bx, t, G_COLS), BF16),
                   jax.ShapeDtypeStruct((bx, t, V_COLS), BF16)],
        compiler_params=_params(2),
        name="inproj_rope" if use_rope else "inproj_ctx",
    )(x, shift, scale, wqk, wg, wv, gain, ones, cos, sin)


def _qk(qs, k):
    return lax.dot_general(qs, k, (((1,), (1,)), ((), ())), preferred_element_type=F32)


def _online_softmax(streams, state, kl_ref, vl_ref, kc_ref, vc_ref, n_lat, tk):
    n = len(streams)
    state = [state[4 * i:4 * i + 4] for i in range(n)]
    for (qs, _, m0, l0), (qs_ref, m_ref, l_ref, acc_ref) in zip(streams, state):
        lane = lax.broadcasted_iota(jnp.int32, m_ref.shape, 1)
        qs_ref[...] = qs
        m_ref[...] = jnp.broadcast_to(m0, m_ref.shape)
        l_ref[...] = jnp.where(lane == 0, l0, 0.0)
        acc_ref[...] = jnp.zeros(acc_ref.shape, F32)

    def scores(i, key_block, width):
        qs = state[i][0][...]
        return [_qk(qs, key_block(c)) for c in range(width // SUB)]

    def update(i, subs, value_block):
        _, m_ref, l_ref, acc_ref = state[i]
        m = m_ref[...]
        top = subs[0]
        for s in subs[1:]:
            top = jnp.maximum(top, s)
        top = functools.reduce(jnp.maximum, [top[:, k * LANES:(k + 1) * LANES] for k in range(SUB // LANES)])
        m_new = jnp.maximum(m, jnp.max(top, axis=-1, keepdims=True))
        alpha = jnp.exp2(m - m_new)
        m_wide = jnp.concatenate([m_new] * (SUB // LANES), axis=1)
        l = alpha * l_ref[...]
        acc = alpha * acc_ref[...]
        for c, s in enumerate(subs):
            p = jnp.exp2(s - m_wide)
            for k in range(SUB // LANES):
                l = l + p[:, k * LANES:(k + 1) * LANES]
            acc = acc + jnp.dot(p.astype(BF16), value_block(c), preferred_element_type=F32)
        m_ref[...] = m_new
        l_ref[...] = l
        acc_ref[...] = acc

    def step(key_block, value_block, width):
        subs = [scores(i, functools.partial(key_block, lanes), width) for i, (_, lanes, _, _) in enumerate(streams)]
        for i, (_, lanes, _, _) in enumerate(streams):
            update(i, subs[i], functools.partial(value_block, lanes))

    if n_lat:
        def body(j, carry):
            rows = lambda c: pl.ds(pl.multiple_of(j * tk + c * SUB, SUB), SUB)
            step(lambda lanes, c: kl_ref[0, rows(c), lanes], lambda lanes, c: vl_ref[0, rows(c), lanes], tk)
            return carry
        lax.fori_loop(0, n_lat, body, 0)
    step(lambda lanes, c: kc_ref[0, c * SUB:(c + 1) * SUB, lanes],
         lambda lanes, c: vc_ref[0, c * SUB:(c + 1) * SUB, lanes], kc_ref.shape[1])
    return [acc_ref[...] / jnp.sum(l_ref[...], axis=-1, keepdims=True) for _, _, l_ref, acc_ref in state]


def _softmax_state(n_streams, rows):
    return [pltpu.VMEM((rows, LANES), BF16), pltpu.VMEM((rows, LANES), F32), pltpu.VMEM((rows, LANES), F32),
            pltpu.VMEM((rows, LANES), F32)] * n_streams


def _stack_pair(q_ref, c, tq, row0=0):
    g = c // 2
    lane = lax.broadcasted_iota(jnp.int32, (tq, LANES), 1)
    keep = (lane >= HEAD_DIM) if g else (lane < HEAD_DIM)
    z = q_ref[0, row0:row0 + tq, c * LANES:(c + 1) * LANES].astype(F32)
    swapped = pltpu.roll(z, HEAD_DIM, 1)
    first, second = (z, swapped) if g == 0 else (swapped, z)
    return jnp.concatenate([jnp.where(keep, first, 0.0).astype(BF16), jnp.where(keep, second, 0.0).astype(BF16)], axis=0)


def _unstack_pair(o, c, tq):
    lane = lax.broadcasted_iota(jnp.int32, (tq, LANES), 1)
    a, b = o[:tq], o[tq:]
    if c // 2 == 0:
        b = pltpu.roll(b, HEAD_DIM, 1)
    else:
        a = pltpu.roll(a, HEAD_DIM, 1)
    return jnp.where(lane < HEAD_DIM, a, b)


def _gqa_kernel(sink_ref, q_ref, kl_ref, vl_ref, kc_ref, vc_ref, o_ref, *state, tq, tk, n_lat, use_sink):
    rs = min(tq, GQA_STREAM_ROWS)
    per_pair = 2 * tq // rs
    streams = []
    for c in range(4):
        pair = _stack_pair(q_ref, c, tq)
        for i in range(per_pair):
            h = 2 * c + i * rs // tq
            if use_sink:
                m0, l0 = jnp.full((rs, 1), sink_ref[h], F32), jnp.ones((rs, 1), F32)
            else:
                m0, l0 = jnp.full((rs, 1), NEG, F32), jnp.zeros((rs, 1), F32)
            streams.append((pair[i * rs:(i + 1) * rs], slice(0, LANES), m0, l0))
    outs = _online_softmax(streams, state, kl_ref, vl_ref, kc_ref, vc_ref, n_lat, tk)
    for c in range(4):
        pair = jnp.concatenate(outs[c * per_pair:(c + 1) * per_pair], axis=0)
        o_ref[0, :, c * LANES:(c + 1) * LANES] = _unstack_pair(pair, c, tq).astype(BF16)


def _diff_kernel(lam_ref, q_ref, kl_ref, vl_ref, kc_ref, vc_ref, gain_ref, o_ref, *state, tq, tk, n_lat, out_scale):
    lane = lax.broadcasted_iota(jnp.int32, (tq, LANES), 1)
    streams = []
    for n in range(4):
        lanes = slice(n * LANES, (n + 1) * LANES)
        z = q_ref[0, :, lanes]
        zero = jnp.zeros_like(z)
        for qs in (jnp.where(lane < HEAD_DIM, z, zero), jnp.where(lane >= HEAD_DIM, z, zero)):
            streams.append((qs, lanes, jnp.full((tq, 1), NEG, F32), jnp.zeros((tq, 1), F32)))
    outs = _online_softmax(streams, state, kl_ref, vl_ref, kc_ref, vc_ref, n_lat, tk)
    for n in range(4):
        d = outs[2 * n] - lam_ref[0] * outs[2 * n + 1]
        d = d * lax.rsqrt(jnp.mean(d * d, axis=-1, keepdims=True) + EPS) * gain_ref[...] * out_scale
        o_ref[0, :, n * LANES:(n + 1) * LANES] = d.astype(BF16)


def _lane_blocks(a):
    return [a[:, k * LANES:(k + 1) * LANES] for k in range(a.shape[1] // LANES)]


def _window_kernel(sink_ref, q_ref, kl_ref, vl_ref, kc_ref, vc_ref, o_ref, *, tq, nb, length):
    span = tq + 2 * WINDOW
    kc = kc_ref[0]
    vc = vc_ref[0]
    rel = lax.broadcasted_iota(jnp.int32, (tq, span), 1) - lax.broadcasted_iota(jnp.int32, (tq, span), 0)
    chains = [(j, h) for j in range(nb) for h in range(8)]
    starts = []
    for j in range(nb):
        first = (pl.program_id(1) * nb + j) * tq
        starts.append((first, pl.multiple_of(jnp.clip(first - WINDOW, 0, length - span), LANES)))
    sw, sc = [], []
    for j in range(nb):
        first, start = starts[j]
        valid = jnp.abs(rel + (start - first)) <= WINDOW
        for c in range(4):
            pair = _stack_pair(q_ref, c, tq, j * tq)
            for qs in (pair[:tq], pair[tq:]):
                sw.append(jnp.where(valid, _qk(qs, kl_ref[0, pl.ds(start, span), :]), NEG))
                sc.append(_qk(qs, kc))
    snk = [jnp.full((tq, 1), sink_ref[h], F32) for _, h in chains]
    top = [jnp.max(functools.reduce(jnp.maximum, _lane_blocks(a) + _lane_blocks(b)), axis=-1, keepdims=True)
           for a, b in zip(sw, sc)]
    m = [jnp.maximum(t, s) for t, s in zip(top, snk)]
    pw = [jnp.exp2(a - mm) for a, mm in zip(sw, m)]
    pc = [jnp.exp2(b - mm) for b, mm in zip(sc, m)]
    tot = [jnp.sum(functools.reduce(jnp.add, _lane_blocks(a) + _lane_blocks(b)), axis=-1, keepdims=True)
           for a, b in zip(pw, pc)]
    l = [t + jnp.exp2(s - mm) for t, s, mm in zip(tot, snk, m)]
    o = [jnp.dot(a.astype(BF16), vl_ref[0, pl.ds(starts[j][1], span), :], preferred_element_type=F32)
         + jnp.dot(b.astype(BF16), vc, preferred_element_type=F32) for (j, _), a, b in zip(chains, pw, pc)]
    o = [a / ll for a, ll in zip(o, l)]
    for j in range(nb):
        for c in range(4):
            pair = jnp.concatenate([o[8 * j + 2 * c], o[8 * j + 2 * c + 1]], axis=0)
            o_ref[0, j * tq:(j + 1) * tq, c * LANES:(c + 1) * LANES] = _unstack_pair(pair, c, tq).astype(BF16)


_SMEM = pl.BlockSpec(memory_space=pltpu.SMEM)


def _q_spec(mixer, rows):
    blk = QK_OFFSET[mixer + "_q"] // MIXER_WIDTH
    return pl.BlockSpec((1, rows, MIXER_WIDTH), lambda b, i: (b, i, blk))


def _kv_specs(klat, kctx, mixer, width):
    s_lat, s_ctx = klat.shape[1], kctx.shape[1]
    k_blk, v_blk = QK_OFFSET[mixer + "_k"] // width, V_OFFSET[mixer + "_v"] // width
    return [pl.BlockSpec((1, s_lat, width), lambda b, i: (b, 0, k_blk)),
            pl.BlockSpec((1, s_lat, width), lambda b, i: (b, 0, v_blk)),
            pl.BlockSpec((1, s_ctx, width), lambda b, i: (b, 0, k_blk)),
            pl.BlockSpec((1, s_ctx, width), lambda b, i: (b, 0, v_blk))]


def _gqa(sink, q, klat, vlat, kctx, vctx, *, mixer, tq, tk, n_lat, use_sink, name):
    bx, t, _ = q.shape
    return pl.pallas_call(
        functools.partial(_gqa_kernel, tq=tq, tk=tk, n_lat=n_lat, use_sink=use_sink),
        grid=(bx, t // tq),
        in_specs=[_SMEM, _q_spec(mixer, tq)] + _kv_specs(klat, kctx, mixer, LANES),
        out_specs=pl.BlockSpec((1, tq, MIXER_WIDTH), lambda b, i: (b, i, 0)),
        out_shape=jax.ShapeDtypeStruct((bx, t, MIXER_WIDTH), BF16),
        scratch_shapes=_softmax_state(8 * tq // min(tq, GQA_STREAM_ROWS), min(tq, GQA_STREAM_ROWS)),
        compiler_params=_params(2),
        name=name,
    )(sink, q, klat, vlat, kctx, vctx)


def _diff(lam, q, klat, vlat, kctx, vctx, gain, *, tq, tk, n_lat, out_scale, name):
    bx, t, _ = q.shape
    return pl.pallas_call(
        functools.partial(_diff_kernel, tq=tq, tk=tk, n_lat=n_lat, out_scale=out_scale),
        grid=(bx, t // tq),
        in_specs=[_SMEM, _q_spec("b", tq)] + _kv_specs(klat, kctx, "b", MIXER_WIDTH)
        + [pl.BlockSpec((1, LANES), lambda b, i: (0, 0))],
        out_specs=pl.BlockSpec((1, tq, MIXER_WIDTH), lambda b, i: (b, i, 0)),
        out_shape=jax.ShapeDtypeStruct((bx, t, MIXER_WIDTH), BF16),
        scratch_shapes=_softmax_state(8, tq),
        compiler_params=_params(2),
        name=name,
    )(lam, q, klat, vlat, kctx, vctx, gain)


def _window(sink, q, klat, vlat, kctx, vctx, *, tq, nb):
    bx, t, _ = q.shape
    return pl.pallas_call(
        functools.partial(_window_kernel, tq=tq, nb=nb, length=t),
        grid=(bx, t // (nb * tq)),
        in_specs=[_SMEM, _q_spec("c", nb * tq)] + _kv_specs(klat, kctx, "c", LANES),
        out_specs=pl.BlockSpec((1, nb * tq, MIXER_WIDTH), lambda b, i: (b, i, 0)),
        out_shape=jax.ShapeDtypeStruct((bx, t, MIXER_WIDTH), BF16),
        compiler_params=_params(2),
        name="attn_window",
    )(sink, q, klat, vlat, kctx, vctx)


def _route(logits, tri, base):
    lane = lax.broadcasted_iota(jnp.int32, logits.shape, 1).astype(F32)
    first = lambda hit: jnp.min(jnp.where(hit, lane, float(LANES)), axis=-1, keepdims=True)
    gl = jnp.where(lane < N_GROUPS, logits, NEG)
    gmax = jnp.max(gl, axis=-1, keepdims=True)
    gidx = first(gl == gmax)
    gw = 1.0 / jnp.sum(jnp.exp(gl - gmax), axis=-1, keepdims=True)
    lo = N_GROUPS + gidx * EXPERTS_PER_GROUP
    el = jnp.where((lane >= lo) & (lane < lo + EXPERTS_PER_GROUP), logits, NEG)
    v1 = jnp.max(el, axis=-1, keepdims=True)
    i1 = first(el == v1)
    el = jnp.where(lane == i1, NEG, el)
    v2 = jnp.max(el, axis=-1, keepdims=True)
    i2 = first(el == v2)
    e = jnp.exp(v2 - v1)
    w1 = gw / (1.0 + e)
    w2 = gw * e / (1.0 + e)
    e1 = i1 - N_GROUPS
    e2 = i2 - N_GROUPS
    hit1 = jnp.where(lane == e1, 1.0, 0.0)
    hit2 = jnp.where(lane == e2, 1.0, 0.0)
    before1 = jnp.dot(tri, hit1.astype(BF16), preferred_element_type=F32) + base
    base = base + jnp.sum(hit1, axis=0, keepdims=True)
    before2 = jnp.dot(tri, hit2.astype(BF16), preferred_element_type=F32) + base
    base = base + jnp.sum(hit2, axis=0, keepdims=True)
    r1 = jnp.sum(hit1 * before1, axis=-1, keepdims=True)
    r2 = jnp.sum(hit2 * before2, axis=-1, keepdims=True)
    out = jnp.zeros_like(logits)
    for k, val in enumerate((e1, e2, w1, w2, r1, r2)):
        out = jnp.where(lane == k, val, out)
    return out, base


def _merge_kernel(ya_ref, yb_ref, yc_ref, g_ref, x_ref, gate_ref, shift_ref, scale_ref,
                  wa_ref, wb_ref, wc_ref, wo_ref, wr_ref, tri_ref, base_ref, xo_ref, h_ref, r_ref, cnt_ref):
    d = D_MODEL

    @pl.when((pl.program_id(0) == 0) & (pl.program_id(1) == 0))
    def _():
        cnt_ref[...] = base_ref[...]

    m = None
    for k, (y_ref, w_ref) in enumerate(((ya_ref, wa_ref), (yb_ref, wb_ref), (yc_ref, wc_ref))):
        t = g_ref[0, :, k * d:(k + 1) * d].astype(F32) * jnp.dot(y_ref[0], w_ref[...], preferred_element_type=F32)
        m = t if m is None else m + t
    xn = x_ref[0] + gate_ref[0] * jnp.dot(m.astype(BF16), wo_ref[...], preferred_element_type=F32)
    xo_ref[0] = xn
    h2 = _rms_modulate(xn, shift_ref[0], scale_ref[0])
    h_ref[0] = h2
    hi = h2.astype(BF16)
    lo = (h2 - hi.astype(F32)).astype(BF16)
    both = jnp.dot(hi, wr_ref[...], preferred_element_type=F32)
    logits = (both[:, :LANES] + both[:, LANES:]) + jnp.dot(lo, wr_ref[:, :LANES], preferred_element_type=F32)
    r_ref[0], cnt_ref[...] = _route(logits, tri_ref[...], cnt_ref[...])


def _merge(ya, yb, yc, g, x, gate, shift, scale, wa, wb, wc, wo, wr, base, *, tm, name):
    bx, t, d = x.shape
    row = lambda b, i: (b, 0, 0)
    tile = lambda b, i: (b, i, 0)
    ids = jnp.arange(tm, dtype=jnp.int32)
    tri = (ids[None, :] < ids[:, None]).astype(BF16)
    return pl.pallas_call(
        _merge_kernel,
        grid=(bx, t // tm),
        in_specs=[pl.BlockSpec((1, tm, 512), tile)] * 3
        + [pl.BlockSpec((1, tm, G_COLS), tile), pl.BlockSpec((1, tm, d), tile)]
        + [pl.BlockSpec((1, 1, d), row)] * 3
        + [_resident((512, d))] * 3 + [_resident((d, d)), _resident((d, 2 * LANES)), _resident((tm, tm)),
                                      _resident((1, LANES))],
        out_specs=[pl.BlockSpec((1, tm, d), tile), pl.BlockSpec((1, tm, d), tile), pl.BlockSpec((1, tm, LANES), tile),
                   pl.BlockSpec((1, LANES), lambda b, i: (0, 0))],
        out_shape=[jax.ShapeDtypeStruct((bx, t, d), F32), jax.ShapeDtypeStruct((bx, t, d), F32),
                   jax.ShapeDtypeStruct((bx, t, LANES), F32), jax.ShapeDtypeStruct((1, LANES), F32)],
        compiler_params=_params(2),
        name=name,
    )(ya, yb, yc, g, x, gate, shift, scale, wa, wb, wc, wo, wr, tri, base)


def _dispatch_kernel(fill_ref, dest_ref, ha_ref, hb_ref, xs_ref, zeros, sem, zero_sem, *, tokens, a_steps, tm):
    i = pl.program_id(0)

    def fill(k):
        return pltpu.make_async_copy(zeros, xs_ref.at[pl.ds(pl.multiple_of(fill_ref[k], tm), tm), :], zero_sem)

    @pl.when(i == 0)
    def _():
        zeros[...] = jnp.zeros(zeros.shape, F32)
        for k in range(fill_ref.shape[0]):
            @pl.when(fill_ref[k] >= 0)
            def _():
                fill(k).start()
        for k in range(fill_ref.shape[0]):
            @pl.when(fill_ref[k] >= 0)
            def _():
                fill(k).wait()

    def scatter(h_ref):
        def issue(r, carry):
            for k in range(2):
                pltpu.make_async_copy(h_ref.at[pl.ds(r, 1), :], xs_ref.at[pl.ds(dest_ref[0, 0, 2 * r + k], 1), :],
                                      sem).start()
            return carry

        lax.fori_loop(0, tokens, issue, 0, unroll=8)
        for _ in range(2):
            pltpu.make_async_copy(h_ref, xs_ref.at[pl.ds(0, tokens), :], sem).wait()

    @pl.when(i < a_steps)
    def _():
        scatter(ha_ref)

    @pl.when(i >= a_steps)
    def _():
        scatter(hb_ref)


def _dispatch(fill, dests, tokens, n_slots, *, td, tm):
    ha, hb = tokens[0], tokens[-1]
    d = ha.shape[1]
    a_steps = ha.shape[0] // td
    steps = a_steps + (hb.shape[0] // td if len(tokens) > 1 else 0)
    dest = dests[0] if len(dests) == 1 else jnp.concatenate(dests)
    return pl.pallas_call(
        functools.partial(_dispatch_kernel, tokens=td, a_steps=a_steps, tm=tm),
        grid_spec=pltpu.PrefetchScalarGridSpec(
            num_scalar_prefetch=1,
            grid=(steps,),
            in_specs=[pl.BlockSpec((1, 1, 2 * td), lambda i, f: (i, 0, 0), memory_space=pltpu.SMEM),
                      pl.BlockSpec((td, d), lambda i, f: (jnp.minimum(i, a_steps - 1), 0)),
                      pl.BlockSpec((td, d), lambda i, f: (jnp.maximum(i - a_steps, 0), 0))],
            out_specs=pl.BlockSpec(memory_space=pl.ANY),
            scratch_shapes=[pltpu.VMEM((tm, d), F32), pltpu.SemaphoreType.DMA(()), pltpu.SemaphoreType.DMA(())]),
        out_shape=jax.ShapeDtypeStruct((n_slots, d), F32),
        compiler_params=_params(1, row_dma=True),
        name="moe_dispatch",
    )(fill, dest.reshape(steps, 1, 2 * td), ha, hb)


def _experts_kernel(te_ref, used_ref, xs_ref, wg_ref, wu_ref, wd_ref, y_ref, wg_bf, wu_bf, wd_bf):
    t = pl.program_id(0)
    live = t < used_ref[0]

    @pl.when(live & ((t == 0) | (te_ref[t] != te_ref[jnp.maximum(t - 1, 0)])))
    def _():
        wg_bf[...] = wg_ref[0, 0].astype(BF16)
        wu_bf[...] = wu_ref[0, 0].astype(BF16)
        wd_bf[...] = wd_ref[0, 0].astype(BF16)

    @pl.when(live)
    def _():
        xb = xs_ref[...].astype(BF16)
        a = jnp.dot(xb, wg_bf[...], preferred_element_type=F32)
        u = jnp.dot(xb, wu_bf[...], preferred_element_type=F32)
        y_ref[...] = jnp.dot((a * _sigmoid(a) * u).astype(BF16), wd_bf[...], preferred_element_type=F32)

    @pl.when(t >= used_ref[0])
    def _():
        y_ref[...] = jnp.zeros_like(y_ref)


def _experts(tile_expert, n_used, xs, wg, wu, wd, *, layer, tm):
    p, d = xs.shape
    hid = wg.shape[3]
    weights = lambda t, te, nu: (layer, te[t], 0, 0)
    return pl.pallas_call(
        _experts_kernel,
        grid_spec=pltpu.PrefetchScalarGridSpec(
            num_scalar_prefetch=2,
            grid=(p // tm,),
            in_specs=[pl.BlockSpec((tm, d), lambda t, te, nu: (t, 0)),
                      pl.BlockSpec((1, 1, d, hid), weights), pl.BlockSpec((1, 1, d, hid), weights),
                      pl.BlockSpec((1, 1, hid, d), weights)],
            out_specs=pl.BlockSpec((tm, d), lambda t, te, nu: (t, 0)),
            scratch_shapes=[pltpu.VMEM((d, hid), BF16), pltpu.VMEM((d, hid), BF16), pltpu.VMEM((hid, d), BF16)]),
        out_shape=jax.ShapeDtypeStruct((p, d), F32),
        compiler_params=_params(1),
        name="moe_experts",
    )(tile_expert, n_used, xs, wg, wu, wd)


def _combine_kernel(dest_ref, next_ref, y_ref, x_ref, gate_ref, r_ref, o_ref, buf, sem, *, tokens, steps):
    i = pl.program_id(0)
    slot = i % 2

    def issue(d_ref, s):
        def body(r, carry):
            for k in range(2):
                pltpu.make_async_copy(y_ref.at[pl.ds(d_ref[0, 0, 2 * r + k], 1), :],
                                      buf.at[s, k, pl.ds(r, 1), :], sem.at[s]).start()
            return carry
        lax.fori_loop(0, tokens, body, 0, unroll=8)

    @pl.when(i == 0)
    def _():
        issue(dest_ref, 0)

    @pl.when(i + 1 < steps)
    def _():
        issue(next_ref, 1 - slot)

    for k in range(2):
        pltpu.make_async_copy(y_ref.at[pl.ds(0, tokens), :], buf.at[slot, k], sem.at[slot]).wait()
    w = r_ref[...]
    o_ref[0] = x_ref[0] + gate_ref[0] * (w[:, 2:3] * buf[slot, 0] + w[:, 3:4] * buf[slot, 1])


def _combine(dest, y, x, gate, route, *, tc, name):
    bx, t, d = x.shape
    per = t // tc
    steps = bx * per
    dest = dest.reshape(steps, 1, 2 * tc)
    return pl.pallas_call(
        functools.partial(_combine_kernel, tokens=tc, steps=steps),
        grid=(steps,),
        in_specs=[pl.BlockSpec((1, 1, 2 * tc), lambda i: (i, 0, 0), memory_space=pltpu.SMEM),
                  pl.BlockSpec((1, 1, 2 * tc), lambda i: (jnp.minimum(i + 1, steps - 1), 0, 0),
                               memory_space=pltpu.SMEM),
                  pl.BlockSpec(memory_space=pl.ANY),
                  pl.BlockSpec((1, tc, d), lambda i: (i // per, i % per, 0)),
                  pl.BlockSpec((1, 1, d), lambda i: (i // per, 0, 0)),
                  pl.BlockSpec((tc, LANES), lambda i: (i, 0))],
        out_specs=pl.BlockSpec((1, tc, d), lambda i: (i // per, i % per, 0)),
        out_shape=jax.ShapeDtypeStruct(x.shape, F32),
        scratch_shapes=[pltpu.VMEM((2, 2, tc, d), F32), pltpu.SemaphoreType.DMA((2,))],
        compiler_params=_params(1, row_dma=True),
        name=name,
    )(dest, dest, y, x, gate, route)


def _expert_slots(routes, counts, tm):
    n_assign = 2 * sum(r.shape[0] for r in routes)
    padded = ((counts + tm - 1) // tm) * tm
    ends = jnp.cumsum(padded)
    starts = ends - padded
    experts = jnp.arange(N_EXPERTS, dtype=jnp.int32)
    dests = []
    for r in routes:
        e = r[:, 0:2].astype(jnp.int32)
        rank = r[:, 4:6].astype(jnp.int32)
        start = jnp.sum(jnp.where(e[:, :, None] == experts, starts, 0), axis=-1)
        dests.append((start + rank).reshape(-1))
    n_tiles = n_assign // tm + N_EXPERTS
    tiles = jnp.arange(n_tiles, dtype=jnp.int32)
    tile_expert = jnp.minimum(jnp.sum((tiles[:, None] >= (ends // tm)[None, :]).astype(jnp.int32), axis=1),
                              N_EXPERTS - 1)
    n_used = (ends[-1:] // tm).astype(jnp.int32)
    unused = n_used + jnp.arange(N_EXPERTS, dtype=jnp.int32)
    fill = jnp.concatenate([jnp.where(padded > 0, ends - tm, -1), jnp.where(unused < n_tiles, unused * tm, -1)])
    return dests, tile_expert, n_used, n_tiles, fill.astype(jnp.int32)


def _rope_tables(length):
    pairs = HEAD_DIM // 4
    pos = jnp.arange(length, dtype=jnp.int32)
    row = (pos // GRID_W).astype(F32)
    col = (pos % GRID_W).astype(F32)
    freqs = ROPE_THETA ** (-jnp.arange(pairs, dtype=F32) / pairs)
    ang = jnp.concatenate([row[:, None] * freqs, col[:, None] * freqs], axis=-1)
    cos = jnp.tile(jnp.cos(ang), (1, 4))
    sin = jnp.tile(jnp.concatenate([-jnp.sin(ang), jnp.sin(ang)], axis=-1), (1, 2))
    return cos, sin


def kernel(x, c, ctx, c_ctx, w_ada, b_ada, w_in, a_qnorm, a_knorm, b_qnorm, b_knorm, c_qnorm, c_knorm, lambda_q1, lambda_k1, lambda_q2, lambda_k2, b_subln, c_sink, w_branch_a, w_branch_b, w_branch_c, w_out, w_router_group, w_router_expert, w_exp_gate, w_exp_up, w_exp_down):
    bsz, length, d = x.shape
    c_len = ctx.shape[1]
    depth = w_ada.shape[0]
    assert d == D_MODEL and bsz + 1 <= MOD_ROWS and length % 512 == 0 and c_len == 256

    cos, sin = _rope_tables(length)
    no_rope = jnp.zeros((c_len, LANES), F32)
    head_ids = jnp.arange(NORM_CHUNK, dtype=jnp.int32) // HEAD_DIM
    ones = (head_ids[:, None] == head_ids[None, :]).astype(BF16)
    c_rows = jnp.concatenate([c, c_ctx[None, :], jnp.zeros((MOD_ROWS - bsz - 1, d), F32)], axis=0)
    mod_all = _ada(c_rows, w_ada, b_ada.reshape(depth, 1, -1))
    zero_sink = jnp.zeros((8,), F32)

    tq_a, tq_b = 256, 256
    tk = 2048 if length % 2048 == 0 else 512
    n_lat = length // tk
    tm_moe = 512

    for l in range(depth):
        need_ctx = l < depth - 1
        mod = lambda k: mod_all[l, :bsz, k * d:(k + 1) * d].reshape(bsz, 1, d)
        cmod = lambda k: jnp.broadcast_to(mod_all[l, bsz, k * d:(k + 1) * d], (bsz, 1, d))
        lam_init = 0.8 - 0.6 * math.exp(-0.3 * l)
        lam = (jnp.exp(jnp.sum(lambda_q1[l] * lambda_k1[l])) - jnp.exp(jnp.sum(lambda_q2[l] * lambda_k2[l]))
               + lam_init).reshape(1).astype(F32)

        w = w_in[l]
        cols = lambda name: w[:, W_IN_COLS[name][0]:W_IN_COLS[name][1]]
        wqk = jnp.concatenate([cols(n) for n in QK_ORDER], axis=1).astype(BF16)
        wg = cols("gates").astype(BF16)
        wv = jnp.concatenate([cols(n) for n in V_ORDER], axis=1).astype(BF16)
        q_scale = HEAD_DIM ** -0.5 * LOG2E
        norm_gain = {"a_q": a_qnorm[l] * q_scale, "b_q": b_qnorm[l] * q_scale, "c_q": c_qnorm[l] * q_scale,
                     "b_k": b_knorm[l], "a_k": a_knorm[l], "c_k": c_knorm[l]}
        gain = jnp.concatenate([jnp.tile(norm_gain[n], (W_IN_COLS[n][1] - W_IN_COLS[n][0]) // HEAD_DIM)
                                for n in QK_ORDER]).reshape(1, QK_COLS)
        subln = b_subln[l].reshape(1, LANES)
        sink = c_sink[l] * LOG2E
        w_router = jnp.concatenate([w_router_group[l], w_router_expert[l],
                                    jnp.zeros((d, LANES - N_GROUPS - N_EXPERTS), F32)], axis=1)
        w_router_hi = w_router.astype(BF16)
        w_router_lo = (w_router - w_router_hi.astype(F32)).astype(BF16)
        merge_w = (w_branch_a[l].astype(BF16), w_branch_b[l].astype(BF16), w_branch_c[l].astype(BF16),
                   w_out[l].astype(BF16), jnp.concatenate([w_router_hi, w_router_lo], axis=1))

        qk, gates, v = _inproj(x, mod(0), mod(1), wqk, wg, wv, gain, ones, cos, sin, use_rope=True, tm=512)
        cqk, cgates, cv = _inproj(ctx, cmod(0), cmod(1), wqk, wg, wv, gain, ones, no_rope, no_rope,
                                  use_rope=False, tm=c_len)
        ya = _gqa(zero_sink, qk, qk, v, cqk, cv, mixer="a", tq=tq_a, tk=tk, n_lat=n_lat,
                  use_sink=False, name="attn_gqa")
        yb = _diff(lam, qk, qk, v, cqk, cv, subln, tq=tq_b, tk=tk, n_lat=n_lat, out_scale=1.0 - lam_init,
                   name="attn_diff")
        yc = _window(sink, qk, qk, v, cqk, cv, tq=128, nb=4)
        x, h2, route, counts = _merge(ya, yb, yc, gates, x, mod(2), mod(3), mod(4), *merge_w,
                                      jnp.zeros((1, LANES), F32), tm=512, name="merge")
        tokens = [h2.reshape(bsz * length, d)]
        routes = [route.reshape(bsz * length, LANES)]
        if need_ctx:
            cya = _gqa(zero_sink, cqk, cqk, cv, cqk, cv, mixer="a", tq=128, tk=tk, n_lat=0,
                       use_sink=False, name="ctx_gqa")
            cyb = _diff(lam, cqk, cqk, cv, cqk, cv, subln, tq=c_len, tk=tk, n_lat=0, out_scale=1.0 - lam_init,
                        name="ctx_diff")
            cyc = _gqa(sink, cqk, cqk, cv, cqk, cv, mixer="c", tq=128, tk=tk, n_lat=0,
                       use_sink=True, name="ctx_sink")
            ctx, hc2, croute, counts = _merge(cya, cyb, cyc, cgates, ctx, cmod(2), cmod(3), cmod(4), *merge_w, counts,
                                              tm=c_len, name="ctx_merge")
            tokens.append(hc2.reshape(bsz * c_len, d))
            routes.append(croute.reshape(bsz * c_len, LANES))

        dests, tile_expert, n_used, n_tiles, fill = _expert_slots(routes, counts[0, :N_EXPERTS].astype(jnp.int32),
                                                                  tm_moe)
        td = 1024 if all(t.shape[0] % 1024 == 0 for t in tokens) else 512
        xs = _dispatch(fill, dests, tokens, n_tiles * tm_moe, td=td, tm=tm_moe)
        y = _experts(tile_expert, n_used, xs, w_exp_gate, w_exp_up, w_exp_down, layer=l, tm=tm_moe)
        x = _combine(dests[0], y, x, mod(5), routes[0], tc=1024 if length % 1024 == 0 else 512, name="moe_combine")
        if need_ctx:
            ctx = _combine(dests[1], y, ctx, cmod(5), routes[1], tc=c_len, name="ctx_combine")
    return x
```
